```python
import math
import jax, jax.numpy as jnp
from jax import lax
import numpy as np


D_MODEL = 2048
BATCH = 1
SEQ = 16384
DEPTH = 4

HEAD_DIM = 128
ROPE_THETA = 10000.0
RMS_EPS = 1e-6
Q_BLOCK = 128
GRID_W = 64
MLA_HEADS = 4
MLA_Q_LORA = 512
MLA_KV_LORA = 512
MLA_NOPE = 128
MLA_ROPE = 64
MLA_V = 128
DIFF_HEADS = 4
DIFF_QK = HEAD_DIM // 2
SWA_HEADS = 4
SWA_KV_HEADS = 2
SWA_WINDOW = 128
SWA_BLOCK = 128
AX_HEADS = 4
AX_KV_HEADS = 2
N_BRANCH = 4
BRANCH_W = 4 * HEAD_DIM
FFN_DIM = ((8 * D_MODEL + 3 * 256 - 1) // (3 * 256)) * 256
PLE_DIM = 256

MLA_COLS = MLA_Q_LORA + MLA_KV_LORA + MLA_ROPE
DIFF_COLS = 3 * DIFF_HEADS * HEAD_DIM
SWA_COLS = (SWA_HEADS + 2 * SWA_KV_HEADS) * HEAD_DIM
AX_COLS = (AX_HEADS + 2 * AX_KV_HEADS) * HEAD_DIM
IN_COLS = MLA_COLS + DIFF_COLS + SWA_COLS + AX_COLS
IN_SPLITS = (MLA_COLS, MLA_COLS + DIFF_COLS, MLA_COLS + DIFF_COLS + SWA_COLS)

kernel_name = 'hybrid_parallel_gated_encoder'


def _rmsnorm(x, g):
    xf = x.astype(jnp.float32)
    y = xf * lax.rsqrt(jnp.mean(xf * xf, axis=-1, keepdims=True) + RMS_EPS)
    return (y * g.astype(jnp.float32)).astype(x.dtype)


def _rope_tables(pos, dim):
    inv_freq = ROPE_THETA ** (-jnp.arange(0, dim, 2, dtype=jnp.float32) / dim)
    ang = pos.astype(jnp.float32)[:, None] * inv_freq[None, :]
    return jnp.cos(ang), jnp.sin(ang)


def _apply_rope(x, cs):
    cos, sin = cs
    c = cos[None, :, None, :].astype(x.dtype)
    s = sin[None, :, None, :].astype(x.dtype)
    x1, x2 = jnp.split(x, 2, axis=-1)
    return jnp.concatenate([x1 * c - x2 * s, x2 * c + x1 * s], axis=-1)


def _axial_rope(x, row_cs, col_cs):
    xr, xc = jnp.split(x, 2, axis=-1)
    return jnp.concatenate([_apply_rope(xr, row_cs), _apply_rope(xc, col_cs)], axis=-1)


def _to_blocks(t):
    b, s = t.shape[:2]
    return jnp.swapaxes(t.reshape(b, s // Q_BLOCK, Q_BLOCK, *t.shape[2:]), 0, 1)


def _from_blocks(t):
    nb, b, qb = t.shape[:3]
    return jnp.swapaxes(t, 0, 1).reshape(b, nb * qb, *t.shape[3:])


def _blocked_attention(q, k, v, scale):
    b, s, hq, dk = q.shape
    hkv = k.shape[2]
    qg = q.reshape(b, s, hkv, hq // hkv, dk)

    def step(qb):
        sc = jnp.einsum('bqhgd,bkhd->bhgqk', qb, k).astype(jnp.float32) * scale
        pr = jax.nn.softmax(sc, axis=-1).astype(v.dtype)
        return jnp.einsum('bhgqk,bkhe->bqhge', pr, v)

    out = _from_blocks(lax.map(step, _to_blocks(qg)))
    return out.reshape(b, s, hq * v.shape[-1])


def _banded_attention(q, k, v, sink, scale):
    b, s, hq, d = q.shape
    hkv = k.shape[2]
    g = hq // hkv
    wb = SWA_BLOCK
    nb = s // wb
    qb = q.reshape(b, nb, wb, hkv, g, d)

    def band(t):
        tb = t.reshape(b, nb, wb, hkv, t.shape[-1])
        tp = jnp.pad(tb, ((0, 0), (1, 1), (0, 0), (0, 0), (0, 0)))
        return jnp.concatenate([tp[:, :-2], tp[:, 1:-1], tp[:, 2:]], axis=2)

    kb, vb = band(k), band(v)
    sc = jnp.einsum('bnqhgd,bnkhd->bnhgqk', qb, kb).astype(jnp.float32) * scale
    blk = jnp.arange(nb)
    qpos = blk[:, None] * wb + jnp.arange(wb)[None, :]
    kpos = blk[:, None] * wb - wb + jnp.arange(3 * wb)[None, :]
    rel = kpos[:, None, :] - qpos[:, :, None]
    valid = (jnp.abs(rel) <= SWA_WINDOW) & (kpos[:, None, :] >= 0) & (kpos[:, None, :] < s)
    sc = jnp.where(valid[None, :, None, None], sc, -jnp.inf)
    sink_col = jnp.broadcast_to(sink.astype(jnp.float32).reshape(1, 1, hkv, g, 1, 1), (b, nb, hkv, g, wb, 1))
    pr = jax.nn.softmax(jnp.concatenate([sc, sink_col], axis=-1), axis=-1)[..., :-1]
    out = jnp.einsum('bnhgqk,bnkhd->bnqhgd', pr.astype(v.dtype), vb)
    return out.reshape(b, s, hq * d)


def _mla(z, qa_norm, w_uq, kva_norm, w_ukv, rope_cs):
    b, s, _ = z.shape
    c_q, c_kv, k_r = jnp.split(z, [MLA_Q_LORA, MLA_Q_LORA + MLA_KV_LORA], axis=-1)
    q = (_rmsnorm(c_q, qa_norm) @ w_uq).reshape(b, s, MLA_HEADS, MLA_NOPE + MLA_ROPE)
    q = jnp.concatenate([q[..., :MLA_NOPE], _apply_rope(q[..., MLA_NOPE:], rope_cs)], axis=-1)
    kv = (_rmsnorm(c_kv, kva_norm) @ w_ukv).reshape(b, s, MLA_HEADS, MLA_NOPE + MLA_V)
    k_r = _apply_rope(k_r.reshape(b, s, 1, MLA_ROPE), rope_cs)
    k = jnp.concatenate([kv[..., :MLA_NOPE], jnp.broadcast_to(k_r, (b, s, MLA_HEADS, MLA_ROPE))], axis=-1)
    v = kv[..., MLA_NOPE:]
    return _blocked_attention(q, k, v, (MLA_NOPE + MLA_ROPE) ** -0.5)


def _diff(z, lam_params, subln, lam_init, rope_cs):
    b, s, _ = z.shape
    hw = DIFF_HEADS * HEAD_DIM
    q, k, v = jnp.split(z, [hw, 2 * hw], axis=-1)
    q = _apply_rope(q.reshape(b, s, DIFF_HEADS * 2, DIFF_QK), rope_cs).reshape(b, s, DIFF_HEADS, 2, DIFF_QK)
    k = _apply_rope(k.reshape(b, s, DIFF_HEADS * 2, DIFF_QK), rope_cs).reshape(b, s, DIFF_HEADS, 2, DIFF_QK)
    v = v.reshape(b, s, DIFF_HEADS, HEAD_DIM)
    q1, q2 = q[:, :, :, 0], q[:, :, :, 1]
    k1, k2 = k[:, :, :, 0], k[:, :, :, 1]
    lp = lam_params.astype(jnp.float32)
    lam = jnp.exp(jnp.sum(lp[0] * lp[1])) - jnp.exp(jnp.sum(lp[2] * lp[3])) + lam_init
    scale = DIFF_QK ** -0.5

    def step(blk):
        q1b, q2b = blk
        a1 = jax.nn.softmax(jnp.einsum('bqhd,bkhd->bhqk', q1b, k1).astype(jnp.float32) * scale, axis=-1)
        a2 = jax.nn.softmax(jnp.einsum('bqhd,bkhd->bhqk', q2b, k2).astype(jnp.float32) * scale, axis=-1)
        return jnp.einsum('bhqk,bkhe->bqhe', (a1 - lam * a2).astype(v.dtype), v)

    o = _from_blocks(lax.map(step, (_to_blocks(q1), _to_blocks(q2))))
    o = _rmsnorm(o, subln) * (1.0 - lam_init)
    return o.reshape(b, s, DIFF_HEADS * HEAD_DIM)


def _swa(z, sink, rope_cs):
    b, s, _ = z.shape
    q, k, v = jnp.split(z, [SWA_HEADS * HEAD_DIM, (SWA_HEADS + SWA_KV_HEADS) * HEAD_DIM], axis=-1)
    q = _apply_rope(q.reshape(b, s, SWA_HEADS, HEAD_DIM), rope_cs)
    k = _apply_rope(k.reshape(b, s, SWA_KV_HEADS, HEAD_DIM), rope_cs)
    v = v.reshape(b, s, SWA_KV_HEADS, HEAD_DIM)
    return _banded_attention(q, k, v, sink, HEAD_DIM ** -0.5)


def _axial(z, q_norm, k_norm, row_cs, col_cs):
    b, s, _ = z.shape
    q, k, v = jnp.split(z, [AX_HEADS * HEAD_DIM, (AX_HEADS + AX_KV_HEADS) * HEAD_DIM], axis=-1)
    q = _axial_rope(_rmsnorm(q.reshape(b, s, AX_HEADS, HEAD_DIM), q_norm), row_cs, col_cs)
    k = _axial_rope(_rmsnorm(k.reshape(b, s, AX_KV_HEADS, HEAD_DIM), k_norm), row_cs, col_cs)
    v = v.reshape(b, s, AX_KV_HEADS, HEAD_DIM)
    return _blocked_attention(q, k, v, HEAD_DIM ** -0.5)


def setup_inputs(seed: int = 0) -> dict:
    key = jax.random.key(seed)
    ks = jax.random.split(key, 24)
    f32 = jnp.float32

    def dense(k, shape, fan_in):
        return jax.random.normal(k, shape, f32) * (fan_in ** -0.5)

    def gain(k, shape):
        return 1.0 + 0.05 * jax.random.normal(k, shape, f32)

    return {
        'x': jax.random.normal(ks[0], (BATCH, SEQ, D_MODEL), f32),
        'p': jax.random.normal(ks[1], (DEPTH, BATCH, SEQ, PLE_DIM), f32),
        'norm_mix_pre': gain(ks[2], (DEPTH, D_MODEL)),
        'norm_mix_post': gain(ks[3], (DEPTH, D_MODEL)),
        'norm_ffn_pre': gain(ks[4], (DEPTH, D_MODEL)),
        'norm_ffn_post': gain(ks[5], (DEPTH, D_MODEL)),
        'norm_ple_post': gain(ks[6], (DEPTH, D_MODEL)),
        'w_in': dense(ks[7], (DEPTH, D_MODEL, IN_COLS), D_MODEL),
        'mla_qa_norm': gain(ks[8], (DEPTH, MLA_Q_LORA)),
        'mla_w_uq': dense(ks[9], (DEPTH, MLA_Q_LORA, MLA_HEADS * (MLA_NOPE + MLA_ROPE)), MLA_Q_LORA),
        'mla_kva_norm': gain(ks[10], (DEPTH, MLA_KV_LORA)),
        'mla_w_ukv': dense(ks[11], (DEPTH, MLA_KV_LORA, MLA_HEADS * (MLA_NOPE + MLA_V)), MLA_KV_LORA),
        'diff_lambda': 0.1 * jax.random.normal(ks[12], (DEPTH, 4, DIFF_QK), f32),
        'diff_subln': gain(ks[13], (DEPTH, HEAD_DIM)),
        'swa_sink': 0.5 * jax.random.normal(ks[14], (DEPTH, SWA_HEADS), f32),
        'ax_q_norm': gain(ks[15], (DEPTH, HEAD_DIM)),
        'ax_k_norm': gain(ks[16], (DEPTH, HEAD_DIM)),
        'w_branch': dense(ks[17], (DEPTH, N_BRANCH, BRANCH_W, D_MODEL), BRANCH_W),
        'w_branch_gate': dense(ks[18], (DEPTH, N_BRANCH, D_MODEL, D_MODEL), D_MODEL),
        'w_o': dense(ks[19], (DEPTH, D_MODEL, D_MODEL), D_MODEL),
        'w_ffn_in': dense(ks[20], (DEPTH, D_MODEL, 2 * FFN_DIM), D_MODEL),
        'w_ffn_out': dense(ks[21], (DEPTH, FFN_DIM, D_MODEL), FFN_DIM),
        'w_ple': dense(ks[22], (DEPTH, PLE_DIM, D_MODEL), PLE_DIM),
        'w_ple_gate': dense(ks[23], (DEPTH, D_MODEL, D_MODEL), D_MODEL),
    }


def reference(x, p, norm_mix_pre, norm_mix_post, norm_ffn_pre, norm_ffn_post, norm_ple_post,
              w_in, mla_qa_norm, mla_w_uq, mla_kva_norm, mla_w_ukv, diff_lambda, diff_subln,
              swa_sink, ax_q_norm, ax_k_norm, w_branch, w_branch_gate, w_o,
              w_ffn_in, w_ffn_out, w_ple, w_ple_gate):
    b, s, _ = x.shape
    rows = s // GRID_W
    t = jnp.arange(s)
    row_pos = jnp.broadcast_to(jnp.arange(rows)[:, None], (rows, GRID_W)).reshape(s)
    col_pos = jnp.broadcast_to(jnp.arange(GRID_W)[None, :], (rows, GRID_W)).reshape(s)
    rope_mla = _rope_tables(t, MLA_ROPE)
    rope_diff = _rope_tables(t, DIFF_QK)
    rope_swa = _rope_tables(t, HEAD_DIM)
    rope_row = _rope_tables(row_pos, HEAD_DIM // 2)
    rope_col = _rope_tables(col_pos, HEAD_DIM // 2)

    for i in range(DEPTH):
        lam_init = 0.8 - 0.6 * math.exp(-0.3 * i)
        h = _rmsnorm(x, norm_mix_pre[i])
        z_a, z_b, z_c, z_d = jnp.split(h @ w_in[i], IN_SPLITS, axis=-1)
        branches = (
            _mla(z_a, mla_qa_norm[i], mla_w_uq[i], mla_kva_norm[i], mla_w_ukv[i], rope_mla),
            _diff(z_b, diff_lambda[i], diff_subln[i], lam_init, rope_diff),
            _swa(z_c, swa_sink[i], rope_swa),
            _axial(z_d, ax_q_norm[i], ax_k_norm[i], rope_row, rope_col),
        )
        merged = jax.nn.sigmoid(h @ w_branch_gate[i, 0]) * (branches[0] @ w_branch[i, 0])
        for j in range(1, N_BRANCH):
            merged = merged + jax.nn.sigmoid(h @ w_branch_gate[i, j]) * (branches[j] @ w_branch[i, j])
        x = x + _rmsnorm(merged @ w_o[i], norm_mix_post[i])
        h = _rmsnorm(x, norm_ffn_pre[i])
        gate, up = jnp.split(h @ w_ffn_in[i], 2, axis=-1)
        x = x + _rmsnorm((jax.nn.silu(gate) * up) @ w_ffn_out[i], norm_ffn_post[i])
        ple = jax.nn.sigmoid(x @ w_ple_gate[i]) * (p[i] @ w_ple[i])
        x = x + _rmsnorm(ple, norm_ple_post[i])
    return x
```

```python
import functools
import math

import jax
import jax.numpy as jnp
from jax import lax
from jax.experimental import pallas as pl
from jax.experimental.pallas import tpu as pltpu

HEAD_DIM = 128
ROPE_THETA = 10000.0
RMS_EPS = 1e-6
GRID_W = 64
MLA_HEADS = 4
MLA_Q_LORA = 512
MLA_KV_LORA = 512
MLA_NOPE = 128
MLA_ROPE = 64
MLA_V = 128
DIFF_HEADS = 4
DIFF_QK = HEAD_DIM // 2
SWA_HEADS = 4
SWA_KV_HEADS = 2
SWA_WINDOW = 128
AX_HEADS = 4
AX_KV_HEADS = 2
N_BRANCH = 4

LANES = 128
MLA_QK_PAD = 2 * LANES
LOG2E = 1.4426950408889634
NEG_BIG = -1e30
VMEM_LIMIT = 56 * 1024 * 1024

BF16 = jnp.bfloat16
F32 = jnp.float32


def _cparams(n_grid):
    return pltpu.CompilerParams(
        dimension_semantics=("arbitrary",) * n_grid,
        vmem_limit_bytes=VMEM_LIMIT,
    )


def _rms(y, g):
    return y * lax.rsqrt(jnp.mean(y * y, axis=-1, keepdims=True) + RMS_EPS) * g


def _rope(x, c, s1, s2, half):
    return (x * c + pltpu.roll(x, LANES - half, 1) * s1
            + pltpu.roll(x, half, 1) * s2)


def _sigmoid(x):
    return 1.0 / (1.0 + jnp.exp(-x))


def _position_tables(s):
    t = jnp.arange(s, dtype=F32)
    rows = s // GRID_W
    row_pos = jnp.broadcast_to(jnp.arange(rows)[:, None], (rows, GRID_W)).reshape(s).astype(F32)
    col_pos = jnp.broadcast_to(jnp.arange(GRID_W)[None, :], (rows, GRID_W)).reshape(s).astype(F32)
    inv64 = ROPE_THETA ** (-jnp.arange(0, 64, 2, dtype=F32) / 64)
    inv128 = ROPE_THETA ** (-jnp.arange(0, 128, 2, dtype=F32) / 128)
    lane = jnp.arange(LANES)
    first32 = (lane % 64) < 32
    first64 = lane < 64

    def tables(ang, first):
        c, sn = jnp.cos(ang), jnp.sin(ang)
        return (c, jnp.where(first[None, :], -sn, 0.0), jnp.where(first[None, :], 0.0, sn))

    ang64 = t[:, None] * inv64[lane % 32][None, :]
    ang128 = t[:, None] * inv128[lane % 64][None, :]
    pos_ax = jnp.where(first64[None, :], row_pos[:, None], col_pos[:, None])
    ang_ax = pos_ax * inv64[lane % 32][None, :]
    return tables(ang64, first32), tables(ang128, first64), tables(ang_ax, first32)


def _prenorm_kernel(x_ref, g_ref, o_ref):
    o_ref[...] = _rms(x_ref[...], g_ref[...]).astype(BF16)


def _prenorm(x, g, tm=512):
    s, d = x.shape
    tm = min(tm, s)
    return pl.pallas_call(
        _prenorm_kernel,
        grid=(s // tm,),
        in_specs=[pl.BlockSpec((tm, d), lambda i: (i, 0)),
                  pl.BlockSpec((1, d), lambda i: (0, 0))],
        out_specs=pl.BlockSpec((tm, d), lambda i: (i, 0)),
        out_shape=jax.ShapeDtypeStruct((s, d), BF16),
        compiler_params=_cparams(1),
        name="prenorm",
    )(x, g.reshape(1, d))


def _inproj_kernel(h_ref, w_ref, c_ref, s1_ref, s2_ref, g_ref, o_ref, *, plan, half):
    acc = jnp.dot(h_ref[...], w_ref[...], preferred_element_type=F32)
    c, s1, s2 = c_ref[...], s1_ref[...], s2_ref[...]
    for b, (kind, gi) in enumerate(plan):
        blk = acc[:, b * LANES:(b + 1) * LANES]
        if kind == "normrope":
            blk = _rms(blk, g_ref[gi:gi + 1, :])
        if kind in ("rope", "normrope"):
            blk = _rope(blk, c, s1, s2, half)
        o_ref[:, b * LANES:(b + 1) * LANES] = blk.astype(BF16)


def _inproj(h, w, tabs, gains, plan, half, tm=512):
    s, d = h.shape
    n = w.shape[1]
    tm = min(tm, s)
    c, s1, s2 = tabs
    ng = gains.shape[0]
    tab_spec = pl.BlockSpec((tm, LANES), lambda i: (i, 0))
    return pl.pallas_call(
        functools.partial(_inproj_kernel, plan=plan, half=half),
        grid=(s // tm,),
        in_specs=[pl.BlockSpec((tm, d), lambda i: (i, 0)),
                  pl.BlockSpec((d, n), lambda i: (0, 0)),
                  tab_spec, tab_spec, tab_spec,
                  pl.BlockSpec((ng, LANES), lambda i: (0, 0))],
        out_specs=pl.BlockSpec((tm, n), lambda i: (i, 0)),
        out_shape=jax.ShapeDtypeStruct((s, n), BF16),
        compiler_params=_cparams(1),
        name="inproj",
    )(h, w, c, s1, s2, gains)


def _mla_in_kernel(h_ref, w_ref, c_ref, s1_ref, s2_ref, gq_ref, gkv_ref,
                   cq_ref, ckv_ref, kr_ref):
    acc = jnp.dot(h_ref[...], w_ref[...], preferred_element_type=F32)
    cq_ref[...] = _rms(acc[:, :MLA_Q_LORA], gq_ref[...]).astype(BF16)
    ckv_ref[...] = _rms(acc[:, MLA_Q_LORA:MLA_Q_LORA + MLA_KV_LORA], gkv_ref[...]).astype(BF16)
    kr = acc[:, MLA_Q_LORA + MLA_KV_LORA:]
    kr_ref[...] = _rope(kr, c_ref[...], s1_ref[...], s2_ref[...], MLA_ROPE // 2).astype(BF16)


def _mla_in(h, w, tabs, gq, gkv, tm=512):
    s, d = h.shape
    n = w.shape[1]
    tm = min(tm, s)
    c, s1, s2 = tabs
    tab_spec = pl.BlockSpec((tm, LANES), lambda i: (i, 0))
    return pl.pallas_call(
        _mla_in_kernel,
        grid=(s // tm,),
        in_specs=[pl.BlockSpec((tm, d), lambda i: (i, 0)),
                  pl.BlockSpec((d, n), lambda i: (0, 0)),
                  tab_spec, tab_spec, tab_spec,
                  pl.BlockSpec((1, MLA_Q_LORA), lambda i: (0, 0)),
                  pl.BlockSpec((1, MLA_KV_LORA), lambda i: (0, 0))],
        out_specs=[pl.BlockSpec((tm, MLA_Q_LORA), lambda i: (i, 0)),
                   pl.BlockSpec((tm, MLA_KV_LORA), lambda i: (i, 0)),
                   pl.BlockSpec((tm, LANES), lambda i: (i, 0))],
        out_shape=[jax.ShapeDtypeStruct((s, MLA_Q_LORA), BF16),
                   jax.ShapeDtypeStruct((s, MLA_KV_LORA), BF16),
                   jax.ShapeDtypeStruct((s, LANES), BF16)],
        compiler_params=_cparams(1),
        name="mla_in",
    )(h, w, c, s1, s2, gq.reshape(1, -1), gkv.reshape(1, -1))


def _mla_up_kernel(cq_ref, ckv_ref, kr_ref, wq_ref, wk_ref, wv_ref,
                   c_ref, s1_ref, s2_ref, q_ref, k_ref, v_ref):
    c, s1, s2 = c_ref[...], s1_ref[...], s2_ref[...]
    q = jnp.dot(cq_ref[...], wq_ref[...], preferred_element_type=F32)
    kn = jnp.dot(ckv_ref[...], wk_ref[...], preferred_element_type=F32)
    kr = kr_ref[...]
    for hd in range(MLA_HEADS):
        lo = hd * MLA_QK_PAD
        q_ref[:, lo:lo + LANES] = q[:, lo:lo + LANES].astype(BF16)
        q_ref[:, lo + LANES:lo + 2 * LANES] = _rope(
            q[:, lo + LANES:lo + 2 * LANES], c, s1, s2, MLA_ROPE // 2).astype(BF16)
        k_ref[:, lo:lo + LANES] = kn[:, hd * LANES:(hd + 1) * LANES].astype(BF16)
        k_ref[:, lo + LANES:lo + 2 * LANES] = kr
    v_ref[...] = jnp.dot(ckv_ref[...], wv_ref[...], preferred_element_type=F32).astype(BF16)


def _mla_up(cq, ckv, kr, wq, wk, wv, tabs, tm=512):
    s = cq.shape[0]
    tm = min(tm, s)
    c, s1, s2 = tabs
    nq = MLA_HEADS * MLA_QK_PAD
    nv = MLA_HEADS * MLA_V
    tab_spec = pl.BlockSpec((tm, LANES), lambda i: (i, 0))
    return pl.pallas_call(
        _mla_up_kernel,
        grid=(s // tm,),
        in_specs=[pl.BlockSpec((tm, MLA_Q_LORA), lambda i: (i, 0)),
                  pl.BlockSpec((tm, MLA_KV_LORA), lambda i: (i, 0)),
                  pl.BlockSpec((tm, LANES), lambda i: (i, 0)),
                  pl.BlockSpec(wq.shape, lambda i: (0, 0)),
                  pl.BlockSpec(wk.shape, lambda i: (0, 0)),
                  pl.BlockSpec(wv.shape, lambda i: (0, 0)),
                  tab_spec, tab_spec, tab_spec],
        out_specs=[pl.BlockSpec((tm, nq), lambda i: (i, 0)),
                   pl.BlockSpec((tm, nq), lambda i: (i, 0)),
                   pl.BlockSpec((tm, nv), lambda i: (i, 0))],
        out_shape=[jax.ShapeDtypeStruct((s, nq), BF16),
                   jax.ShapeDtypeStruct((s, nq), BF16),
                   jax.ShapeDtypeStruct((s, nv), BF16)],
        compiler_params=_cparams(1),
        name="mla_up",
    )(cq, ckv, kr, wq, wk, wv, c, s1, s2)


def _flash_kernel(*refs, mode, tq, tk, scale2, lam_init):
    if mode == "diff":
        q_ref, k_ref, vt_ref, lam_ref, subln_ref, o_ref = refs
    else:
        q_ref, k_ref, vt_ref, o_ref = refs
    q = q_ref[...]
    if mode == "single":
        qcat = q
    elif mode == "pair":
        qcat = jnp.concatenate([q[:, :LANES], q[:, LANES:]], axis=0)
    else:
        lane = lax.broadcasted_iota(jnp.int32, q.shape, 1)
        zero = jnp.zeros_like(q)
        qcat = jnp.concatenate([jnp.where(lane < DIFF_QK, q, zero),
                                jnp.where(lane >= DIFF_QK, q, zero)], axis=0)
    ncols = qcat.shape[0]
    dv = vt_ref.shape[1]
    nk = vt_ref.shape[0]

    def body(ci, carry):
        m, l, acc = carry
        k0 = pl.multiple_of(ci * tk, tk)
        kc = k_ref[pl.ds(k0, tk), :]
        sc = lax.dot_general(kc, qcat, (((1,), (1,)), ((), ())),
                             preferred_element_type=F32) * scale2
        m_new = jnp.maximum(m, jnp.max(sc, axis=0, keepdims=True))
        alpha = jnp.exp2(m - m_new)
        p = jnp.exp2(sc - m_new)
        l = alpha * l + jnp.sum(p, axis=0, keepdims=True)
        acc = alpha * acc + jnp.dot(vt_ref[ci], p.astype(BF16), preferred_element_type=F32)
        return m_new, l, acc

    init = (jnp.full((1, ncols), NEG_BIG, F32), jnp.zeros((1, ncols), F32),
            jnp.zeros((dv, ncols), F32))
    _, l, acc = lax.fori_loop(0, nk, body, init)
    ot = acc * (1.0 / l)
    if mode == "single":
        o_ref[...] = ot.T.astype(BF16)
    elif mode == "pair":
        o_ref[:, :dv] = ot[:, :tq].T.astype(BF16)
        o_ref[:, dv:] = ot[:, tq:].T.astype(BF16)
    else:
        lp = lam_ref[...]
        lam = (jnp.exp(jnp.sum(lp[0:1] * lp[1:2], axis=1, keepdims=True))
               - jnp.exp(jnp.sum(lp[2:3] * lp[3:4], axis=1, keepdims=True)) + lam_init)
        d = (ot[:, :tq] - lam * ot[:, tq:]).T
        o_ref[...] = (_rms(d, subln_ref[...]) * (1.0 - lam_init)).astype(BF16)


def _chunked_vt(v, heads, tk):
    s = v.shape[0]
    dv = v.shape[1] // heads
    return jnp.transpose(v.reshape(s // tk, tk, heads, dv), (2, 0, 3, 1))


def _flash(q_arr, k_arr, vt, *, mode, n_steps, q_blk, q_off, k_blk, k_off, out_w,
           scale, tq=512, tk=512, lam=None, subln=None, lam_init=0.0):
    s = q_arr.shape[0]
    tq = min(tq, s)
    _, nk, dv, tk = vt.shape
    in_specs = [pl.BlockSpec((tq, q_blk), lambda h, i: (i, h + q_off)),
                pl.BlockSpec((s, k_blk), lambda h, i: (0, h + k_off)),
                pl.BlockSpec((None, nk, dv, tk), lambda h, i: (h, 0, 0, 0))]
    args = [q_arr, k_arr, vt]
    if mode == "diff":
        in_specs += [pl.BlockSpec(lam.shape, lambda h, i: (0, 0)),
                     pl.BlockSpec((1, HEAD_DIM), lambda h, i: (0, 0))]
        args += [lam, subln.reshape(1, HEAD_DIM)]
    return pl.pallas_call(
        functools.partial(_flash_kernel, mode=mode, tq=tq, tk=tk,
                          scale2=scale * LOG2E, lam_init=lam_init),
        grid=(n_steps, s // tq),
        in_specs=in_specs,
        out_specs=pl.BlockSpec((tq, out_w), lambda h, i: (i, h)),
        out_shape=jax.ShapeDtypeStruct((s, n_steps * out_w), BF16),
        compiler_params=_cparams(2),
        name="flash_" + mode,
    )(*args)


def _swa_kernel(sink_ref, q_ref, k_ref, v_ref, o_ref, *, tq, win, scale):
    g = pl.program_id(0)
    i = pl.program_id(1)
    s = k_ref.shape[0]
    q0 = i * tq
    k0 = pl.multiple_of(jnp.clip(q0 - SWA_WINDOW, 0, s - win), LANES)
    kw = k_ref[pl.ds(k0, win), :]
    vw = v_ref[pl.ds(k0, win), :]
    qpos = q0 + lax.broadcasted_iota(jnp.int32, (tq, win), 0)
    kpos = k0 + lax.broadcasted_iota(jnp.int32, (tq, win), 1)
    valid = jnp.abs(kpos - qpos) <= SWA_WINDOW
    group = SWA_HEADS // SWA_KV_HEADS
    for j in range(group):
        q = q_ref[:, j * HEAD_DIM:(j + 1) * HEAD_DIM]
        sc = lax.dot_general(q, kw, (((1,), (1,)), ((), ())),
                             preferred_element_type=F32) * scale
        sc = jnp.where(valid, sc, NEG_BIG)
        sink = sink_ref[g * group + j]
        m = jnp.maximum(jnp.max(sc, axis=1, keepdims=True), sink)
        p = jnp.exp(sc - m)
        denom = jnp.sum(p, axis=1, keepdims=True) + jnp.exp(sink - m)
        pr = (p * (1.0 / denom)).astype(BF16)
        o_ref[:, j * HEAD_DIM:(j + 1) * HEAD_DIM] = jnp.dot(
            pr, vw, preferred_element_type=F32).astype(BF16)


def _swa(z, sink, tq=512):
    s = z.shape[0]
    tq = min(tq, s)
    win = min(tq + 2 * SWA_WINDOW, s)
    group = SWA_HEADS // SWA_KV_HEADS
    qw = group * HEAD_DIM
    k_off = SWA_HEADS
    v_off = SWA_HEADS + SWA_KV_HEADS
    return pl.pallas_call(
        functools.partial(_swa_kernel, tq=tq, win=win, scale=HEAD_DIM ** -0.5),
        grid=(SWA_KV_HEADS, s // tq),
        in_specs=[pl.BlockSpec(memory_space=pltpu.SMEM),
                  pl.BlockSpec((tq, qw), lambda g, i: (i, g)),
                  pl.BlockSpec((s, HEAD_DIM), lambda g, i: (0, g + k_off)),
                  pl.BlockSpec((s, HEAD_DIM), lambda g, i: (0, g + v_off))],
        out_specs=pl.BlockSpec((tq, qw), lambda g, i: (i, g)),
        out_shape=jax.ShapeDtypeStruct((s, SWA_HEADS * HEAD_DIM), BF16),
        compiler_params=_cparams(2),
        name="swa",
    )(sink, z, z, z)


def _merge_kernel(h_ref, b0_ref, b1_ref, b2_ref, b3_ref, wg_ref, wb_ref, o_ref):
    h = h_ref[...]
    acc = None
    for j, b_ref in enumerate((b0_ref, b1_ref, b2_ref, b3_ref)):
        gate = _sigmoid(jnp.dot(h, wg_ref[j], preferred_element_type=F32))
        term = gate * jnp.dot(b_ref[...], wb_ref[j], preferred_element_type=F32)
        acc = term if acc is None else acc + term
    o_ref[...] = acc.astype(BF16)


def _merge(h, branches, wg, wb, tm=512, tn=512):
    s, d = h.shape
    n = wg.shape[2]
    bw = wb.shape[1]
    tm = min(tm, s)
    b_spec = pl.BlockSpec((tm, bw), lambda j, i: (i, 0))
    return pl.pallas_call(
        _merge_kernel,
        grid=(n // tn, s // tm),
        in_specs=[pl.BlockSpec((tm, d), lambda j, i: (i, 0)),
                  b_spec, b_spec, b_spec, b_spec,
                  pl.BlockSpec((N_BRANCH, d, tn), lambda j, i: (0, 0, j)),
                  pl.BlockSpec((N_BRANCH, bw, tn), lambda j, i: (0, 0, j))],
        out_specs=pl.BlockSpec((tm, tn), lambda j, i: (i, j)),
        out_shape=jax.ShapeDtypeStruct((s, n), BF16),
        compiler_params=_cparams(2),
        name="merge",
    )(h, *branches, wg, wb)


def _attn_out_kernel(m_ref, w_ref, x_ref, gpost_ref, gnext_ref, xo_ref, ho_ref):
    y = jnp.dot(m_ref[...], w_ref[...], preferred_element_type=F32)
    xn = x_ref[...] + _rms(y, gpost_ref[...])
    xo_ref[...] = xn
    ho_ref[...] = _rms(xn, gnext_ref[...]).astype(BF16)


def _attn_out(merged, w, x, gpost, gnext, tm=256):
    s, d = x.shape
    tm = min(tm, s)
    row = pl.BlockSpec((tm, d), lambda i: (i, 0))
    vec = pl.BlockSpec((1, d), lambda i: (0, 0))
    return pl.pallas_call(
        _attn_out_kernel,
        grid=(s // tm,),
        in_specs=[row, pl.BlockSpec((d, d), lambda i: (0, 0)), row, vec, vec],
        out_specs=[row, row],
        out_shape=[jax.ShapeDtypeStruct((s, d), F32), jax.ShapeDtypeStruct((s, d), BF16)],
        compiler_params=_cparams(1),
        name="attn_out",
    )(merged, w, x, gpost.reshape(1, d), gnext.reshape(1, d))


def _ffn_in_kernel(h_ref, wg_ref, wu_ref, o_ref):
    h = h_ref[...]
    gate = jnp.dot(h, wg_ref[...], preferred_element_type=F32)
    up = jnp.dot(h, wu_ref[...], preferred_element_type=F32)
    o_ref[...] = (gate * _sigmoid(gate) * up).astype(BF16)


def _ffn_in(h, w, tm=1024, tn=512):
    s, d = h.shape
    f = w.shape[1] // 2
    tm = min(tm, s)
    nt = f // tn
    return pl.pallas_call(
        _ffn_in_kernel,
        grid=(nt, s // tm),
        in_specs=[pl.BlockSpec((tm, d), lambda j, i: (i, 0)),
                  pl.BlockSpec((d, tn), lambda j, i: (0, j)),
                  pl.BlockSpec((d, tn), lambda j, i: (0, j + nt))],
        out_specs=pl.BlockSpec((tm, tn), lambda j, i: (i, j)),
        out_shape=jax.ShapeDtypeStruct((s, f), BF16),
        compiler_params=_cparams(2),
        name="ffn_in",
    )(h, w, w)


def _ffn_out_kernel(u_ref, w_ref, x_ref, g_ref, xo_ref, xb_ref, acc_ref):
    k = pl.program_id(1)
    part = jnp.dot(u_ref[...], w_ref[...], preferred_element_type=F32)

    @pl.when(k == 0)
    def _():
        acc_ref[...] = part

    @pl.when(k > 0)
    def _():
        acc_ref[...] += part

    @pl.when(k == pl.num_programs(1) - 1)
    def _():
        xn = x_ref[...] + _rms(acc_ref[...], g_ref[...])
        xo_ref[...] = xn
        xb_ref[...] = xn.astype(BF16)


def _ffn_out(u, w, x, g, tm=512, n_k=4):
    s, d = x.shape
    f = u.shape[1]
    tkk = f // n_k
    tm = min(tm, s)
    row = pl.BlockSpec((tm, d), lambda i, k: (i, 0))
    return pl.pallas_call(
        _ffn_out_kernel,
        grid=(s // tm, n_k),
        in_specs=[pl.BlockSpec((tm, tkk), lambda i, k: (i, k)),
                  pl.BlockSpec((tkk, d), lambda i, k: (k, 0)),
                  row, pl.BlockSpec((1, d), lambda i, k: (0, 0))],
        out_specs=[row, row],
        out_shape=[jax.ShapeDtypeStruct((s, d), F32), jax.ShapeDtypeStruct((s, d), BF16)],
        scratch_shapes=[pltpu.VMEM((tm, d), F32)],
        compiler_params=_cparams(2),
        name="ffn_out",
    )(u, w, x, g.reshape(1, d))


def _ple_kernel(xb_ref, wg_ref, p_ref, wp_ref, x_ref, gpost_ref, gnext_ref, xo_ref, ho_ref):
    gate = _sigmoid(jnp.dot(xb_ref[...], wg_ref[...], preferred_element_type=F32))
    emb = jnp.dot(p_ref[...].astype(BF16), wp_ref[...], preferred_element_type=F32)
    xn = x_ref[...] + _rms(gate * emb, gpost_ref[...])
    xo_ref[...] = xn
    ho_ref[...] = _rms(xn, gnext_ref[...]).astype(BF16)


def _ple(xb, wg, p, wp, x, gpost, gnext, tm=256):
    s, d = x.shape
    pd = p.shape[1]
    tm = min(tm, s)
    row = pl.BlockSpec((tm, d), lambda i: (i, 0))
    vec = pl.BlockSpec((1, d), lambda i: (0, 0))
    return pl.pallas_call(
        _ple_kernel,
        grid=(s // tm,),
        in_specs=[row, pl.BlockSpec((d, d), lambda i: (0, 0)),
                  pl.BlockSpec((tm, pd), lambda i: (i, 0)),
                  pl.BlockSpec((pd, d), lambda i: (0, 0)),
                  row, vec, vec],
        out_specs=[row, row],
        out_shape=[jax.ShapeDtypeStruct((s, d), F32), jax.ShapeDtypeStruct((s, d), BF16)],
        compiler_params=_cparams(1),
        name="ple",
    )(xb, wg, p, wp, x, gpost.reshape(1, d), gnext.reshape(1, d))


def _pad_cols(w, to):
    return jnp.pad(w, ((0, 0), (0, to - w.shape[1])))


def kernel(x, p, norm_mix_pre, norm_mix_post, norm_ffn_pre, norm_ffn_post, norm_ple_post, w_in, mla_qa_norm, mla_w_uq, mla_kva_norm, mla_w_ukv, diff_lambda, diff_subln, swa_sink, ax_q_norm, ax_k_norm, w_branch, w_branch_gate, w_o, w_ffn_in, w_ffn_out, w_ple, w_ple_gate):
    b, s, d = x.shape
    assert b == 1
    depth = w_in.shape[0]
    x2 = x.reshape(s, d)
    tab64, tab128, tab_ax = _position_tables(s)
    tk = min(512, s)

    mla_cols = MLA_Q_LORA + MLA_KV_LORA + MLA_ROPE
    diff_cols = 3 * DIFF_HEADS * HEAD_DIM
    swa_cols = (SWA_HEADS + 2 * SWA_KV_HEADS) * HEAD_DIM
    c1 = mla_cols
    c2 = c1 + diff_cols
    c3 = c2 + swa_cols

    diff_plan = tuple([("rope", 0)] * 8 + [("none", 0)] * 4)
    swa_plan = tuple([("rope", 0)] * 6 + [("none", 0)] * 2)
    ax_plan = tuple([("normrope", 0)] * 4 + [("normrope", 1)] * 2 + [("none", 0)] * 2)
    no_gain = jnp.ones((1, LANES), F32)

    h = _prenorm(x2, norm_mix_pre[0])
    for i in range(depth):
        lam_init = 0.8 - 0.6 * math.exp(-0.3 * i)
        wi = w_in[i].astype(BF16)
        w_mla = _pad_cols(wi[:, :c1], MLA_Q_LORA + MLA_KV_LORA + LANES)
        w_diff, w_swa, w_ax = wi[:, c1:c2], wi[:, c2:c3], wi[:, c3:]

        cq, ckv, kr = _mla_in(h, w_mla, tab64, mla_qa_norm[i], mla_kva_norm[i])
        wq = jnp.pad(mla_w_uq[i].reshape(MLA_Q_LORA, MLA_HEADS, MLA_NOPE + MLA_ROPE),
                     ((0, 0), (0, 0), (0, MLA_QK_PAD - MLA_NOPE - MLA_ROPE))
                     ).reshape(MLA_Q_LORA, MLA_HEADS * MLA_QK_PAD).astype(BF16)
        wkv = mla_w_ukv[i].reshape(MLA_KV_LORA, MLA_HEADS, MLA_NOPE + MLA_V)
        wk = wkv[:, :, :MLA_NOPE].reshape(MLA_KV_LORA, MLA_HEADS * MLA_NOPE).astype(BF16)
        wv = wkv[:, :, MLA_NOPE:].reshape(MLA_KV_LORA, MLA_HEADS * MLA_V).astype(BF16)
        qa, ka, va = _mla_up(cq, ckv, kr, wq, wk, wv, tab64)
        br_a = _flash(qa, ka, _chunked_vt(va, MLA_HEADS, tk), mode="single",
                      n_steps=MLA_HEADS, q_blk=MLA_QK_PAD, q_off=0, k_blk=MLA_QK_PAD, k_off=0,
                      out_w=MLA_V, scale=(MLA_NOPE + MLA_ROPE) ** -0.5)

        zb = _inproj(h, w_diff, tab64, no_gain, diff_plan, DIFF_QK // 2)
        vb = zb[:, 2 * DIFF_HEADS * HEAD_DIM:]
        br_b = _flash(zb, zb, _chunked_vt(vb, DIFF_HEADS, tk), mode="diff",
                      n_steps=DIFF_HEADS, q_blk=HEAD_DIM, q_off=0, k_blk=HEAD_DIM,
                      k_off=DIFF_HEADS, out_w=HEAD_DIM, scale=DIFF_QK ** -0.5,
                      lam=diff_lambda[i], subln=diff_subln[i], lam_init=lam_init)

        zc = _inproj(h, w_swa, tab128, no_gain, swa_plan, HEAD_DIM // 2)
        br_c = _swa(zc, swa_sink[i])

        gains_ax = jnp.stack([ax_q_norm[i], ax_k_norm[i]])
        zd = _inproj(h, w_ax, tab_ax, gains_ax, ax_plan, HEAD_DIM // 4)
        vd = zd[:, (AX_HEADS + AX_KV_HEADS) * HEAD_DIM:]
        group = AX_HEADS // AX_KV_HEADS
        br_d = _flash(zd, zd, _chunked_vt(vd, AX_KV_HEADS, tk), mode="pair",
                      n_steps=AX_KV_HEADS, q_blk=group * HEAD_DIM, q_off=0, k_blk=HEAD_DIM,
                      k_off=AX_HEADS, out_w=group * HEAD_DIM, scale=HEAD_DIM ** -0.5)

        merged = _merge(h, (br_a, br_b, br_c, br_d),
                        w_branch_gate[i].astype(BF16), w_branch[i].astype(BF16))
        x2, h2 = _attn_out(merged, w_o[i].astype(BF16), x2, norm_mix_post[i], norm_ffn_pre[i])

        u = _ffn_in(h2, w_ffn_in[i].astype(BF16))
        x2, xb = _ffn_out(u, w_ffn_out[i].astype(BF16), x2, norm_ffn_post[i])

        g_next = norm_mix_pre[i + 1] if i + 1 < depth else norm_mix_pre[i]
        x2, h = _ple(xb, w_ple_gate[i].astype(BF16), p[i, 0], w_ple[i].astype(BF16), x2,
                     norm_ple_post[i], g_next)
    return x2.reshape(b, s, d)
```

```python
import functools
import math

import jax
import jax.numpy as jnp
from jax import lax
from jax.experimental import pallas as pl
from jax.experimental.pallas import tpu as pltpu

HEAD_DIM = 128
ROPE_THETA = 10000.0
RMS_EPS = 1e-6
GRID_W = 64
MLA_HEADS = 4
MLA_Q_LORA = 512
MLA_KV_LORA = 512
MLA_NOPE = 128
MLA_ROPE = 64
MLA_V = 128
DIFF_HEADS = 4
DIFF_QK = HEAD_DIM // 2
SWA_HEADS = 4
SWA_KV_HEADS = 2
SWA_WINDOW = 128
AX_HEADS = 4
AX_KV_HEADS = 2
N_BRANCH = 4

LANES = 128
SUBLANES = 8
BF16_ROWS = 16
MLA_QK_PAD = 2 * LANES
LOG2E = 1.4426950408889634
NEG_BIG = -1e30
MLA_QSCALE = (MLA_NOPE + MLA_ROPE) ** -0.5 * LOG2E
FLASH_TK = 1024
VMEM_LIMIT = 56 * 1024 * 1024

BF16 = jnp.bfloat16
F32 = jnp.float32


def _cparams(n_grid):
    return pltpu.CompilerParams(
        dimension_semantics=("arbitrary",) * n_grid,
        vmem_limit_bytes=VMEM_LIMIT,
    )


def _rms(y, g):
    return y * lax.rsqrt(jnp.mean(y * y, axis=-1, keepdims=True) + RMS_EPS) * g


def _rope(x, c, s1, s2, half):
    return (x * c + pltpu.roll(x, LANES - half, 1) * s1
            + pltpu.roll(x, half, 1) * s2)


def _sigmoid(x):
    return 1.0 / (1.0 + jnp.exp(-x))


def _position_tables(s):
    t = jnp.arange(s, dtype=F32)
    rows = s // GRID_W
    row_pos = jnp.broadcast_to(jnp.arange(rows)[:, None], (rows, GRID_W)).reshape(s).astype(F32)
    col_pos = jnp.broadcast_to(jnp.arange(GRID_W)[None, :], (rows, GRID_W)).reshape(s).astype(F32)
    inv64 = ROPE_THETA ** (-jnp.arange(0, 64, 2, dtype=F32) / 64)
    inv128 = ROPE_THETA ** (-jnp.arange(0, 128, 2, dtype=F32) / 128)
    lane = jnp.arange(LANES)
    first32 = (lane % 64) < 32
    first64 = lane < 64

    def tables(ang, first):
        c, sn = jnp.cos(ang), jnp.sin(ang)
        return (c, jnp.where(first[None, :], -sn, 0.0), jnp.where(first[None, :], 0.0, sn))

    ang64 = t[:, None] * inv64[lane % 32][None, :]
    ang128 = t[:, None] * inv128[lane % 64][None, :]
    pos_ax = jnp.where(first64[None, :], row_pos[:, None], col_pos[:, None])
    ang_ax = pos_ax * inv64[lane % 32][None, :]
    return tables(ang64, first32), tables(ang128, first64), tables(ang_ax, first32)


def _prenorm_kernel(x_ref, g_ref, o_ref):
    o_ref[...] = _rms(x_ref[...], g_ref[...]).astype(BF16)


def _prenorm(x, g, tm=512):
    s, d = x.shape
    tm = min(tm, s)
    return pl.pallas_call(
        _prenorm_kernel,
        grid=(s // tm,),
        in_specs=[pl.BlockSpec((tm, d), lambda i: (i, 0)),
                  pl.BlockSpec((1, d), lambda i: (0, 0))],
        out_specs=pl.BlockSpec((tm, d), lambda i: (i, 0)),
        out_shape=jax.ShapeDtypeStruct((s, d), BF16),
        compiler_params=_cparams(1),
        name="prenorm",
    )(x, g.reshape(1, d))


def _inproj_kernel(h_ref, w_ref, c_ref, s1_ref, s2_ref, g_ref, o_ref, *, plan, half):
    acc = jnp.dot(h_ref[...], w_ref[...], preferred_element_type=F32)
    c, s1, s2 = c_ref[...], s1_ref[...], s2_ref[...]
    for b, (kind, gi, scale) in enumerate(plan):
        blk = acc[:, b * LANES:(b + 1) * LANES]
        if kind == "normrope":
            blk = _rms(blk, g_ref[gi:gi + 1, :])
        if kind in ("rope", "normrope"):
            blk = _rope(blk, c, s1, s2, half)
        if scale != 1.0:
            blk = blk * scale
        o_ref[:, b * LANES:(b + 1) * LANES] = blk.astype(BF16)


def _inproj(h, w, tabs, gains, plan, half, tm=512):
    s, d = h.shape
    n = w.shape[1]
    tm = min(tm, s)
    c, s1, s2 = tabs
    ng = gains.shape[0]
    tab_spec = pl.BlockSpec((tm, LANES), lambda i: (i, 0))
    return pl.pallas_call(
        functools.partial(_inproj_kernel, plan=plan, half=half),
        grid=(s // tm,),
        in_specs=[pl.BlockSpec((tm, d), lambda i: (i, 0)),
                  pl.BlockSpec((d, n), lambda i: (0, 0)),
                  tab_spec, tab_spec, tab_spec,
                  pl.BlockSpec((ng, LANES), lambda i: (0, 0))],
        out_specs=pl.BlockSpec((tm, n), lambda i: (i, 0)),
        out_shape=jax.ShapeDtypeStruct((s, n), BF16),
        compiler_params=_cparams(1),
        name="inproj",
    )(h, w, c, s1, s2, gains)


def _mla_in_kernel(h_ref, w_ref, c_ref, s1_ref, s2_ref, gq_ref, gkv_ref,
                   cq_ref, ckv_ref, kr_ref):
    acc = jnp.dot(h_ref[...], w_ref[...], preferred_element_type=F32)
    cq_ref[...] = _rms(acc[:, :MLA_Q_LORA], gq_ref[...]).astype(BF16)
    ckv_ref[...] = _rms(acc[:, MLA_Q_LORA:MLA_Q_LORA + MLA_KV_LORA], gkv_ref[...]).astype(BF16)
    kr = acc[:, MLA_Q_LORA + MLA_KV_LORA:]
    kr_ref[...] = _rope(kr, c_ref[...], s1_ref[...], s2_ref[...], MLA_ROPE // 2).astype(BF16)


def _mla_in(h, w, tabs, gq, gkv, tm=512):
    s, d = h.shape
    n = w.shape[1]
    tm = min(tm, s)
    c, s1, s2 = tabs
    tab_spec = pl.BlockSpec((tm, LANES), lambda i: (i, 0))
    return pl.pallas_call(
        _mla_in_kernel,
        grid=(s // tm,),
        in_specs=[pl.BlockSpec((tm, d), lambda i: (i, 0)),
                  pl.BlockSpec((d, n), lambda i: (0, 0)),
                  tab_spec, tab_spec, tab_spec,
                  pl.BlockSpec((1, MLA_Q_LORA), lambda i: (0, 0)),
                  pl.BlockSpec((1, MLA_KV_LORA), lambda i: (0, 0))],
        out_specs=[pl.BlockSpec((tm, MLA_Q_LORA), lambda i: (i, 0)),
                   pl.BlockSpec((tm, MLA_KV_LORA), lambda i: (i, 0)),
                   pl.BlockSpec((tm, LANES), lambda i: (i, 0))],
        out_shape=[jax.ShapeDtypeStruct((s, MLA_Q_LORA), BF16),
                   jax.ShapeDtypeStruct((s, MLA_KV_LORA), BF16),
                   jax.ShapeDtypeStruct((s, LANES), BF16)],
        compiler_params=_cparams(1),
        name="mla_in",
    )(h, w, c, s1, s2, gq.reshape(1, -1), gkv.reshape(1, -1))


def _mla_up_kernel(cq_ref, ckv_ref, kr_ref, wq_ref, wk_ref, wv_ref,
                   c_ref, s1_ref, s2_ref, q_ref, k_ref, v_ref):
    c, s1, s2 = c_ref[...], s1_ref[...], s2_ref[...]
    q = jnp.dot(cq_ref[...], wq_ref[...], preferred_element_type=F32)
    kn = jnp.dot(ckv_ref[...], wk_ref[...], preferred_element_type=F32)
    kr = kr_ref[...]
    for hd in range(MLA_HEADS):
        lo = hd * MLA_QK_PAD
        q_ref[:, lo:lo + LANES] = (q[:, lo:lo + LANES] * MLA_QSCALE).astype(BF16)
        q_ref[:, lo + LANES:lo + 2 * LANES] = (_rope(
            q[:, lo + LANES:lo + 2 * LANES], c, s1, s2, MLA_ROPE // 2) * MLA_QSCALE).astype(BF16)
        k_ref[:, lo:lo + LANES] = kn[:, hd * LANES:(hd + 1) * LANES].astype(BF16)
        k_ref[:, lo + LANES:lo + 2 * LANES] = kr
    v_ref[...] = jnp.dot(ckv_ref[...], wv_ref[...], preferred_element_type=F32).astype(BF16)


def _mla_up(cq, ckv, kr, wq, wk, wv, tabs, tm=512):
    s = cq.shape[0]
    tm = min(tm, s)
    c, s1, s2 = tabs
    nq = MLA_HEADS * MLA_QK_PAD
    nv = MLA_HEADS * MLA_V
    tab_spec = pl.BlockSpec((tm, LANES), lambda i: (i, 0))
    return pl.pallas_call(
        _mla_up_kernel,
        grid=(s // tm,),
        in_specs=[pl.BlockSpec((tm, MLA_Q_LORA), lambda i: (i, 0)),
                  pl.BlockSpec((tm, MLA_KV_LORA), lambda i: (i, 0)),
                  pl.BlockSpec((tm, LANES), lambda i: (i, 0)),
                  pl.BlockSpec(wq.shape, lambda i: (0, 0)),
                  pl.BlockSpec(wk.shape, lambda i: (0, 0)),
                  pl.BlockSpec(wv.shape, lambda i: (0, 0)),
                  tab_spec, tab_spec, tab_spec],
        out_specs=[pl.BlockSpec((tm, nq), lambda i: (i, 0)),
                   pl.BlockSpec((tm, nq), lambda i: (i, 0)),
                   pl.BlockSpec((tm, nv), lambda i: (i, 0))],
        out_shape=[jax.ShapeDtypeStruct((s, nq), BF16),
                   jax.ShapeDtypeStruct((s, nq), BF16),
                   jax.ShapeDtypeStruct((s, nv), BF16)],
        compiler_params=_cparams(1),
        name="mla_up",
    )(cq, ckv, kr, wq, wk, wv, c, s1, s2)


def _flash_kernel(*refs, mode, tq, tk, lam_init):
    if mode == "diff":
        (q_ref, k_ref, vt_ref, lam_ref, subln_ref, o_ref,
         qcat_ref, s0_ref, s1_ref, p_ref, acc_ref) = refs
    else:
        q_ref, k_ref, vt_ref, o_ref, qcat_ref, s0_ref, s1_ref, p_ref, acc_ref = refs
    q = q_ref[...]
    if mode == "single":
        qcat_ref[...] = q
    elif mode == "pair":
        qcat_ref[:tq] = q[:, :LANES]
        qcat_ref[tq:] = q[:, LANES:]
    else:
        lane = lax.broadcasted_iota(jnp.int32, q.shape, 1)
        zero = jnp.zeros_like(q)
        qcat_ref[:tq] = jnp.where(lane < DIFF_QK, q, zero)
        qcat_ref[tq:] = jnp.where(lane >= DIFF_QK, q, zero)
    ncols = qcat_ref.shape[0]
    dv = vt_ref.shape[1]
    nk = vt_ref.shape[0]
    s_bufs = (s0_ref, s1_ref)

    def scores(ci, dst_ref):
        k0 = pl.multiple_of(ci * tk, tk)
        sc = lax.dot_general(k_ref[pl.ds(k0, tk), :], qcat_ref[...], (((1,), (1,)), ((), ())),
                             preferred_element_type=F32)
        dst_ref[...] = sc
        return jnp.max(sc, axis=0, keepdims=True)

    def probs(s_ref, m_new):
        mb = jnp.broadcast_to(m_new, (SUBLANES, ncols))
        psum = jnp.zeros((SUBLANES, ncols), F32)
        for r in range(tk // BF16_ROWS):
            lo = r * BF16_ROWS
            p0 = jnp.exp2(s_ref[lo:lo + SUBLANES, :] - mb)
            p1 = jnp.exp2(s_ref[lo + SUBLANES:lo + BF16_ROWS, :] - mb)
            psum = psum + p0 + p1
            p_ref[lo:lo + BF16_ROWS, :] = jnp.concatenate([p0, p1], axis=0).astype(BF16)
        return psum

    def body(j, carry):
        m, l, cmax = carry
        for b in range(2):
            ci = 2 * j + b
            cmax_next = scores(jnp.minimum(ci + 1, nk - 1), s_bufs[1 - b])
            m_new = jnp.maximum(m, cmax)
            alpha = jnp.exp2(m - m_new)
            psum = probs(s_bufs[b], m_new)
            l = alpha * l + jnp.sum(psum, axis=0, keepdims=True)
            acc_ref[...] = alpha * acc_ref[...] + jnp.dot(
                vt_ref[ci], p_ref[...], preferred_element_type=F32)
            m, cmax = m_new, cmax_next
        return m, l, cmax

    cmax0 = scores(0, s_bufs[0])
    acc_ref[...] = jnp.zeros_like(acc_ref)
    init = (jnp.full((1, ncols), NEG_BIG, F32), jnp.zeros((1, ncols), F32), cmax0)
    _, l, _ = lax.fori_loop(0, nk // 2, body, init)
    ot = acc_ref[...] * (1.0 / l)
    if mode == "single":
        o_ref[...] = ot.T.astype(BF16)
    elif mode == "pair":
        o_ref[:, :dv] = ot[:, :tq].T.astype(BF16)
        o_ref[:, dv:] = ot[:, tq:].T.astype(BF16)
    else:
        lp = lam_ref[...]
        lam = (jnp.exp(jnp.sum(lp[0:1] * lp[1:2], axis=1, keepdims=True))
               - jnp.exp(jnp.sum(lp[2:3] * lp[3:4], axis=1, keepdims=True)) + lam_init)
        d = (ot[:, :tq] - lam * ot[:, tq:]).T
        o_ref[...] = (_rms(d, subln_ref[...]) * (1.0 - lam_init)).astype(BF16)


def _chunked_vt(v, heads, tk):
    s = v.shape[0]
    dv = v.shape[1] // heads
    return jnp.transpose(v.reshape(s // tk, tk, heads, dv), (2, 0, 3, 1))


def _flash(q_arr, k_arr, vt, *, mode, n_steps, q_blk, q_off, k_blk, k_off, out_w,
           tq=512, lam=None, subln=None, lam_init=0.0):
    s = q_arr.shape[0]
    tq = min(tq, s)
    _, nk, dv, tk = vt.shape
    assert nk % 2 == 0
    ncols = tq if mode == "single" else 2 * tq
    in_specs = [pl.BlockSpec((tq, q_blk), lambda h, i: (i, h + q_off)),
                pl.BlockSpec((s, k_blk), lambda h, i: (0, h + k_off)),
                pl.BlockSpec((None, nk, dv, tk), lambda h, i: (h, 0, 0, 0))]
    args = [q_arr, k_arr, vt]
    if mode == "diff":
        in_specs += [pl.BlockSpec(lam.shape, lambda h, i: (0, 0)),
                     pl.BlockSpec((1, HEAD_DIM), lambda h, i: (0, 0))]
        args += [lam, subln.reshape(1, HEAD_DIM)]
    return pl.pallas_call(
        functools.partial(_flash_kernel, mode=mode, tq=tq, tk=tk, lam_init=lam_init),
        grid=(n_steps, s // tq),
        in_specs=in_specs,
        out_specs=pl.BlockSpec((tq, out_w), lambda h, i: (i, h)),
        out_shape=jax.ShapeDtypeStruct((s, n_steps * out_w), BF16),
        scratch_shapes=[pltpu.VMEM((ncols, k_blk), BF16),
                        pltpu.VMEM((tk, ncols), F32), pltpu.VMEM((tk, ncols), F32),
                        pltpu.VMEM((tk, ncols), BF16), pltpu.VMEM((dv, ncols), F32)],
        compiler_params=_cparams(2),
        name="flash_" + mode,
    )(*args)


def _swa_kernel(sink_ref, q_ref, k_ref, v_ref, o_ref, *, tq, win, scale):
    g = pl.program_id(0)
    i = pl.program_id(1)
    s = k_ref.shape[0]
    q0 = i * tq
    k0 = pl.multiple_of(jnp.clip(q0 - SWA_WINDOW, 0, s - win), LANES)
    kw = k_ref[pl.ds(k0, win), :]
    vw = v_ref[pl.ds(k0, win), :]
    qpos = q0 + lax.broadcasted_iota(jnp.int32, (tq, win), 0)
    kpos = k0 + lax.broadcasted_iota(jnp.int32, (tq, win), 1)
    valid = jnp.abs(kpos - qpos) <= SWA_WINDOW
    group = SWA_HEADS // SWA_KV_HEADS
    for j in range(group):
        q = q_ref[:, j * HEAD_DIM:(j + 1) * HEAD_DIM]
        sc = lax.dot_general(q, kw, (((1,), (1,)), ((), ())),
                             preferred_element_type=F32) * scale
        sc = jnp.where(valid, sc, NEG_BIG)
        sink = sink_ref[g * group + j]
        m = jnp.maximum(jnp.max(sc, axis=1, keepdims=True), sink)
        p = jnp.exp(sc - m)
        denom = jnp.sum(p, axis=1, keepdims=True) + jnp.exp(sink - m)
        pr = (p * (1.0 / denom)).astype(BF16)
        o_ref[:, j * HEAD_DIM:(j + 1) * HEAD_DIM] = jnp.dot(
            pr, vw, preferred_element_type=F32).astype(BF16)


def _swa(z, sink, tq=512):
    s = z.shape[0]
    tq = min(tq, s)
    win = min(tq + 2 * SWA_WINDOW, s)
    group = SWA_HEADS // SWA_KV_HEADS
    qw = group * HEAD_DIM
    k_off = SWA_HEADS
    v_off = SWA_HEADS + SWA_KV_HEADS
    return pl.pallas_call(
        functools.partial(_swa_kernel, tq=tq, win=win, scale=HEAD_DIM ** -0.5),
        grid=(SWA_KV_HEADS, s // tq),
        in_specs=[pl.BlockSpec(memory_space=pltpu.SMEM),
                  pl.BlockSpec((tq, qw), lambda g, i: (i, g)),
                  pl.BlockSpec((s, HEAD_DIM), lambda g, i: (0, g + k_off)),
                  pl.BlockSpec((s, HEAD_DIM), lambda g, i: (0, g + v_off))],
        out_specs=pl.BlockSpec((tq, qw), lambda g, i: (i, g)),
        out_shape=jax.ShapeDtypeStruct((s, SWA_HEADS * HEAD_DIM), BF16),
        compiler_params=_cparams(2),
        name="swa",
    )(sink, z, z, z)


def _merge_kernel(h_ref, b0_ref, b1_ref, b2_ref, b3_ref, wg_ref, wb_ref, o_ref):
    h = h_ref[...]
    acc = None
    for j, b_ref in enumerate((b0_ref, b1_ref, b2_ref, b3_ref)):
        gate = _sigmoid(jnp.dot(h, wg_ref[j], preferred_element_type=F32))
        term = gate * jnp.dot(b_ref[...], wb_ref[j], preferred_element_type=F32)
        acc = term if acc is None else acc + term
    o_ref[...] = acc.astype(BF16)


def _merge(h, branches, wg, wb, tm=512, tn=512):
    s, d = h.shape
    n = wg.shape[2]
    bw = wb.shape[1]
    tm = min(tm, s)
    b_spec = pl.BlockSpec((tm, bw), lambda j, i: (i, 0))
    return pl.pallas_call(
        _merge_kernel,
        grid=(n // tn, s // tm),
        in_specs=[pl.BlockSpec((tm, d), lambda j, i: (i, 0)),
                  b_spec, b_spec, b_spec, b_spec,
                  pl.BlockSpec((N_BRANCH, d, tn), lambda j, i: (0, 0, j)),
                  pl.BlockSpec((N_BRANCH, bw, tn), lambda j, i: (0, 0, j))],
        out_specs=pl.BlockSpec((tm, tn), lambda j, i: (i, j)),
        out_shape=jax.ShapeDtypeStruct((s, n), BF16),
        compiler_params=_cparams(2),
        name="merge",
    )(h, *branches, wg, wb)


def _attn_out_kernel(m_ref, w_ref, x_ref, gpost_ref, gnext_ref, xo_ref, ho_ref):
    y = jnp.dot(m_ref[...], w_ref[...], preferred_element_type=F32)
    xn = x_ref[...] + _rms(y, gpost_ref[...])
    xo_ref[...] = xn
    ho_ref[...] = _rms(xn, gnext_ref[...]).astype(BF16)


def _attn_out(merged, w, x, gpost, gnext, tm=256):
    s, d = x.shape
    tm = min(tm, s)
    row = pl.BlockSpec((tm, d), lambda i: (i, 0))
    vec = pl.BlockSpec((1, d), lambda i: (0, 0))
    return pl.pallas_call(
        _attn_out_kernel,
        grid=(s // tm,),
        in_specs=[row, pl.BlockSpec((d, d), lambda i: (0, 0)), row, vec, vec],
        out_specs=[row, row],
        out_shape=[jax.ShapeDtypeStruct((s, d), F32), jax.ShapeDtypeStruct((s, d), BF16)],
        compiler_params=_cparams(1),
        name="attn_out",
    )(merged, w, x, gpost.reshape(1, d), gnext.reshape(1, d))


def _ffn_in_kernel(h_ref, wg_ref, wu_ref, o_ref):
    h = h_ref[...]
    gate = jnp.dot(h, wg_ref[...], preferred_element_type=F32)
    up = jnp.dot(h, wu_ref[...], preferred_element_type=F32)
    o_ref[...] = (gate * _sigmoid(gate) * up).astype(BF16)


def _ffn_in(h, w, tm=1024, tn=512):
    s, d = h.shape
    f = w.shape[1] // 2
    tm = min(tm, s)
    nt = f // tn
    return pl.pallas_call(
        _ffn_in_kernel,
        grid=(nt, s // tm),
        in_specs=[pl.BlockSpec((tm, d), lambda j, i: (i, 0)),
                  pl.BlockSpec((d, tn), lambda j, i: (0, j)),
                  pl.BlockSpec((d, tn), lambda j, i: (0, j + nt))],
        out_specs=pl.BlockSpec((tm, tn), lambda j, i: (i, j)),
        out_shape=jax.ShapeDtypeStruct((s, f), BF16),
        compiler_params=_cparams(2),
        name="ffn_in",
    )(h, w, w)


def _ffn_out_kernel(u_ref, w_ref, x_ref, g_ref, xo_ref, xb_ref, acc_ref):
    k = pl.program_id(1)
    part = jnp.dot(u_ref[...], w_ref[...], preferred_element_type=F32)

    @pl.when(k == 0)
    def _():
        acc_ref[...] = part

    @pl.when(k > 0)
    def _():
        acc_ref[...] += part

    @pl.when(k == pl.num_programs(1) - 1)
    def _():
        xn = x_ref[...] + _rms(acc_ref[...], g_ref[...])
        xo_ref[...] = xn
        xb_ref[...] = xn.astype(BF16)


def _ffn_out(u, w, x, g, tm=512, n_k=4):
    s, d = x.shape
    f = u.shape[1]
    tkk = f // n_k
    tm = min(tm, s)
    row = pl.BlockSpec((tm, d), lambda i, k: (i, 0))
    return pl.pallas_call(
        _ffn_out_kernel,
        grid=(s // tm, n_k),
        in_specs=[pl.BlockSpec((tm, tkk), lambda i, k: (i, k)),
                  pl.BlockSpec((tkk, d), lambda i, k: (k, 0)),
                  row, pl.BlockSpec((1, d), lambda i, k: (0, 0))],
        out_specs=[row, row],
        out_shape=[jax.ShapeDtypeStruct((s, d), F32), jax.ShapeDtypeStruct((s, d), BF16)],
        scratch_shapes=[pltpu.VMEM((tm, d), F32)],
        compiler_params=_cparams(2),
        name="ffn_out",
    )(u, w, x, g.reshape(1, d))


def _ple_kernel(xb_ref, wg_ref, p_ref, wp_ref, x_ref, gpost_ref, gnext_ref, xo_ref, ho_ref):
    gate = _sigmoid(jnp.dot(xb_ref[...], wg_ref[...], preferred_element_type=F32))
    emb = jnp.dot(p_ref[...].astype(BF16), wp_ref[...], preferred_element_type=F32)
    xn = x_ref[...] + _rms(gate * emb, gpost_ref[...])
    xo_ref[...] = xn
    ho_ref[...] = _rms(xn, gnext_ref[...]).astype(BF16)


def _ple(xb, wg, p, wp, x, gpost, gnext, tm=256):
    s, d = x.shape
    pd = p.shape[1]
    tm = min(tm, s)
    row = pl.BlockSpec((tm, d), lambda i: (i, 0))
    vec = pl.BlockSpec((1, d), lambda i: (0, 0))
    return pl.pallas_call(
        _ple_kernel,
        grid=(s // tm,),
        in_specs=[row, pl.BlockSpec((d, d), lambda i: (0, 0)),
                  pl.BlockSpec((tm, pd), lambda i: (i, 0)),
                  pl.BlockSpec((pd, d), lambda i: (0, 0)),
                  row, vec, vec],
        out_specs=[row, row],
        out_shape=[jax.ShapeDtypeStruct((s, d), F32), jax.ShapeDtypeStruct((s, d), BF16)],
        compiler_params=_cparams(1),
        name="ple",
    )(xb, wg, p, wp, x, gpost.reshape(1, d), gnext.reshape(1, d))


def _pad_cols(w, to):
    return jnp.pad(w, ((0, 0), (0, to - w.shape[1])))


def kernel(x, p, norm_mix_pre, norm_mix_post, norm_ffn_pre, norm_ffn_post, norm_ple_post, w_in, mla_qa_norm, mla_w_uq, mla_kva_norm, mla_w_ukv, diff_lambda, diff_subln, swa_sink, ax_q_norm, ax_k_norm, w_branch, w_branch_gate, w_o, w_ffn_in, w_ffn_out, w_ple, w_ple_gate):
    b, s, d = x.shape
    assert b == 1
    depth = w_in.shape[0]
    x2 = x.reshape(s, d)
    tab64, tab128, tab_ax = _position_tables(s)
    tk = min(FLASH_TK, s)

    mla_cols = MLA_Q_LORA + MLA_KV_LORA + MLA_ROPE
    diff_cols = 3 * DIFF_HEADS * HEAD_DIM
    swa_cols = (SWA_HEADS + 2 * SWA_KV_HEADS) * HEAD_DIM
    c1 = mla_cols
    c2 = c1 + diff_cols
    c3 = c2 + swa_cols

    diff_plan = tuple([("rope", 0, DIFF_QK ** -0.5 * LOG2E)] * 4 + [("rope", 0, 1.0)] * 4
                      + [("none", 0, 1.0)] * 4)
    swa_plan = tuple([("rope", 0, 1.0)] * 6 + [("none", 0, 1.0)] * 2)
    ax_plan = tuple([("normrope", 0, HEAD_DIM ** -0.5 * LOG2E)] * 4 + [("normrope", 1, 1.0)] * 2
                    + [("none", 0, 1.0)] * 2)
    no_gain = jnp.ones((1, LANES), F32)

    h = _prenorm(x2, norm_mix_pre[0])
    for i in range(depth):
        lam_init = 0.8 - 0.6 * math.exp(-0.3 * i)
        wi = w_in[i].astype(BF16)
        w_mla = _pad_cols(wi[:, :c1], MLA_Q_LORA + MLA_KV_LORA + LANES)
        w_diff, w_swa, w_ax = wi[:, c1:c2], wi[:, c2:c3], wi[:, c3:]

        cq, ckv, kr = _mla_in(h, w_mla, tab64, mla_qa_norm[i], mla_kva_norm[i])
        wq = jnp.pad(mla_w_uq[i].reshape(MLA_Q_LORA, MLA_HEADS, MLA_NOPE + MLA_ROPE),
                     ((0, 0), (0, 0), (0, MLA_QK_PAD - MLA_NOPE - MLA_ROPE))
                     ).reshape(MLA_Q_LORA, MLA_HEADS * MLA_QK_PAD).astype(BF16)
        wkv = mla_w_ukv[i].reshape(MLA_KV_LORA, MLA_HEADS, MLA_NOPE + MLA_V)
        wk = wkv[:, :, :MLA_NOPE].reshape(MLA_KV_LORA, MLA_HEADS * MLA_NOPE).astype(BF16)
        wv = wkv[:, :, MLA_NOPE:].reshape(MLA_KV_LORA, MLA_HEADS * MLA_V).astype(BF16)
        qa, ka, va = _mla_up(cq, ckv, kr, wq, wk, wv, tab64)
        br_a = _flash(qa, ka, _chunked_vt(va, MLA_HEADS, tk), mode="single",
                      n_steps=MLA_HEADS, q_blk=MLA_QK_PAD, q_off=0, k_blk=MLA_QK_PAD, k_off=0,
                      out_w=MLA_V, tq=1024)

        zb = _inproj(h, w_diff, tab64, no_gain, diff_plan, DIFF_QK // 2)
        vb = zb[:, 2 * DIFF_HEADS * HEAD_DIM:]
        br_b = _flash(zb, zb, _chunked_vt(vb, DIFF_HEADS, tk), mode="diff",
                      n_steps=DIFF_HEADS, q_blk=HEAD_DIM, q_off=0, k_blk=HEAD_DIM,
                      k_off=DIFF_HEADS, out_w=HEAD_DIM,
                      lam=diff_lambda[i], subln=diff_subln[i], lam_init=lam_init)

        zc = _inproj(h, w_swa, tab128, no_gain, swa_plan, HEAD_DIM // 2)
        br_c = _swa(zc, swa_sink[i])

        gains_ax = jnp.stack([ax_q_norm[i], ax_k_norm[i]])
        zd = _inproj(h, w_ax, tab_ax, gains_ax, ax_plan, HEAD_DIM // 4)
        vd = zd[:, (AX_HEADS + AX_KV_HEADS) * HEAD_DIM:]
        group = AX_HEADS // AX_KV_HEADS
        br_d = _flash(zd, zd, _chunked_vt(vd, AX_KV_HEADS, tk), mode="pair",
                      n_steps=AX_KV_HEADS, q_blk=group * HEAD_DIM, q_off=0, k_blk=HEAD_DIM,
                      k_off=AX_HEADS, out_w=group * HEAD_DIM)

        merged = _merge(h, (br_a, br_b, br_c, br_d),
                        w_branch_gate[i].astype(BF16), w_branch[i].astype(BF16))
        x2, h2 = _attn_out(merged, w_o[i].astype(BF16), x2, norm_mix_post[i], norm_ffn_pre[i])

        u = _ffn_in(h2, w_ffn_in[i].astype(BF16))
        x2, xb = _ffn_out(u, w_ffn_out[i].astype(BF16), x2, norm_ffn_post[i])

        g_next = norm_mix_pre[i + 1] if i + 1 < depth else norm_mix_pre[i]
        x2, h = _ple(xb, w_ple_gate[i].astype(BF16), p[i, 0], w_ple[i].astype(BF16), x2,
                     norm_ple_post[i], g_next)
    return x2.reshape(b, s, d)
```

```python
import functools
import math

import jax
import jax.numpy as jnp
from jax import lax
from jax.experimental import pallas as pl
from jax.experimental.pallas import tpu as pltpu

HEAD_DIM = 128
ROPE_THETA = 10000.0
RMS_EPS = 1e-6
GRID_W = 64
MLA_HEADS = 4
MLA_Q_LORA = 512
MLA_KV_LORA = 512
MLA_NOPE = 128
MLA_ROPE = 64
MLA_V = 128
DIFF_HEADS = 4
DIFF_QK = HEAD_DIM // 2
SWA_HEADS = 4
SWA_KV_HEADS = 2
SWA_WINDOW = 128
AX_HEADS = 4
AX_KV_HEADS = 2
N_BRANCH = 4

LANES = 128
SUBLANES = 8
BF16_ROWS = 16
MLA_QK_PAD = 2 * LANES
LOG2E = 1.4426950408889634
NEG_BIG = -1e30
MLA_QSCALE = (MLA_NOPE + MLA_ROPE) ** -0.5 * LOG2E
FLASH_TK = 1024
FLASH_PIECE = 256
VMEM_LIMIT = 56 * 1024 * 1024

BF16 = jnp.bfloat16
F32 = jnp.float32


def _cparams(n_grid):
    return pltpu.CompilerParams(
        dimension_semantics=("arbitrary",) * n_grid,
        vmem_limit_bytes=VMEM_LIMIT,
    )


def _resident(shape):
    return pl.BlockSpec(shape, lambda *_: (0,) * len(shape), pipeline_mode=pl.Buffered(1))


def _rms(y, g):
    return y * lax.rsqrt(jnp.mean(y * y, axis=-1, keepdims=True) + RMS_EPS) * g


def _rope(x, c, s1, s2, half):
    return (x * c + pltpu.roll(x, LANES - half, 1) * s1
            + pltpu.roll(x, half, 1) * s2)


def _sigmoid(x):
    return 1.0 / (1.0 + jnp.exp(-x))


def _position_tables(s):
    t = jnp.arange(s, dtype=F32)
    rows = s // GRID_W
    row_pos = jnp.broadcast_to(jnp.arange(rows)[:, None], (rows, GRID_W)).reshape(s).astype(F32)
    col_pos = jnp.broadcast_to(jnp.arange(GRID_W)[None, :], (rows, GRID_W)).reshape(s).astype(F32)
    inv64 = ROPE_THETA ** (-jnp.arange(0, 64, 2, dtype=F32) / 64)
    inv128 = ROPE_THETA ** (-jnp.arange(0, 128, 2, dtype=F32) / 128)
    lane = jnp.arange(LANES)
    first32 = (lane % 64) < 32
    first64 = lane < 64

    def tables(ang, first):
        c, sn = jnp.cos(ang), jnp.sin(ang)
        return (c, jnp.where(first[None, :], -sn, 0.0), jnp.where(first[None, :], 0.0, sn))

    ang64 = t[:, None] * inv64[lane % 32][None, :]
    ang128 = t[:, None] * inv128[lane % 64][None, :]
    pos_ax = jnp.where(first64[None, :], row_pos[:, None], col_pos[:, None])
    ang_ax = pos_ax * inv64[lane % 32][None, :]
    return tables(ang64, first32), tables(ang128, first64), tables(ang_ax, first32)


def _prenorm_kernel(x_ref, g_ref, o_ref):
    o_ref[...] = _rms(x_ref[...], g_ref[...]).astype(BF16)


def _prenorm(x, g, tm=512):
    s, d = x.shape
    tm = min(tm, s)
    return pl.pallas_call(
        _prenorm_kernel,
        grid=(s // tm,),
        in_specs=[pl.BlockSpec((tm, d), lambda i: (i, 0)),
                  pl.BlockSpec((1, d), lambda i: (0, 0))],
        out_specs=pl.BlockSpec((tm, d), lambda i: (i, 0)),
        out_shape=jax.ShapeDtypeStruct((s, d), BF16),
        compiler_params=_cparams(1),
        name="prenorm",
    )(x, g.reshape(1, d))


def _inproj_kernel(h_ref, w_ref, c_ref, s1_ref, s2_ref, g_ref, o_ref, *, plan, half):
    acc = jnp.dot(h_ref[...], w_ref[...], preferred_element_type=F32)
    c, s1, s2 = c_ref[...], s1_ref[...], s2_ref[...]
    for b, (kind, gi, scale) in enumerate(plan):
        blk = acc[:, b * LANES:(b + 1) * LANES]
        if kind == "normrope":
            blk = _rms(blk, g_ref[gi:gi + 1, :])
        if kind in ("rope", "normrope"):
            blk = _rope(blk, c, s1, s2, half)
        if scale != 1.0:
            blk = blk * scale
        o_ref[:, b * LANES:(b + 1) * LANES] = blk.astype(BF16)


def _inproj(h, w, tabs, gains, plan, half, tm=512):
    s, d = h.shape
    n = w.shape[1]
    tm = min(tm, s)
    c, s1, s2 = tabs
    ng = gains.shape[0]
    tab_spec = pl.BlockSpec((tm, LANES), lambda i: (i, 0))
    return pl.pallas_call(
        functools.partial(_inproj_kernel, plan=plan, half=half),
        grid=(s // tm,),
        in_specs=[pl.BlockSpec((tm, d), lambda i: (i, 0)),
                  pl.BlockSpec((d, n), lambda i: (0, 0)),
                  tab_spec, tab_spec, tab_spec,
                  pl.BlockSpec((ng, LANES), lambda i: (0, 0))],
        out_specs=pl.BlockSpec((tm, n), lambda i: (i, 0)),
        out_shape=jax.ShapeDtypeStruct((s, n), BF16),
        compiler_params=_cparams(1),
        name="inproj",
    )(h, w, c, s1, s2, gains)


def _mla_in_kernel(h_ref, w_ref, c_ref, s1_ref, s2_ref, gq_ref, gkv_ref,
                   cq_ref, ckv_ref, kr_ref):
    acc = jnp.dot(h_ref[...], w_ref[...], preferred_element_type=F32)
    cq_ref[...] = _rms(acc[:, :MLA_Q_LORA], gq_ref[...]).astype(BF16)
    ckv_ref[...] = _rms(acc[:, MLA_Q_LORA:MLA_Q_LORA + MLA_KV_LORA], gkv_ref[...]).astype(BF16)
    kr = acc[:, MLA_Q_LORA + MLA_KV_LORA:]
    kr_ref[...] = _rope(kr, c_ref[...], s1_ref[...], s2_ref[...], MLA_ROPE // 2).astype(BF16)


def _mla_in(h, w, tabs, gq, gkv, tm=512):
    s, d = h.shape
    n = w.shape[1]
    tm = min(tm, s)
    c, s1, s2 = tabs
    tab_spec = pl.BlockSpec((tm, LANES), lambda i: (i, 0))
    return pl.pallas_call(
        _mla_in_kernel,
        grid=(s // tm,),
        in_specs=[pl.BlockSpec((tm, d), lambda i: (i, 0)),
                  pl.BlockSpec((d, n), lambda i: (0, 0)),
                  tab_spec, tab_spec, tab_spec,
                  pl.BlockSpec((1, MLA_Q_LORA), lambda i: (0, 0)),
                  pl.BlockSpec((1, MLA_KV_LORA), lambda i: (0, 0))],
        out_specs=[pl.BlockSpec((tm, MLA_Q_LORA), lambda i: (i, 0)),
                   pl.BlockSpec((tm, MLA_KV_LORA), lambda i: (i, 0)),
                   pl.BlockSpec((tm, LANES), lambda i: (i, 0))],
        out_shape=[jax.ShapeDtypeStruct((s, MLA_Q_LORA), BF16),
                   jax.ShapeDtypeStruct((s, MLA_KV_LORA), BF16),
                   jax.ShapeDtypeStruct((s, LANES), BF16)],
        compiler_params=_cparams(1),
        name="mla_in",
    )(h, w, c, s1, s2, gq.reshape(1, -1), gkv.reshape(1, -1))


def _mla_up_kernel(cq_ref, ckv_ref, kr_ref, wq_ref, wk_ref, wv_ref,
                   c_ref, s1_ref, s2_ref, q_ref, k_ref, v_ref):
    c, s1, s2 = c_ref[...], s1_ref[...], s2_ref[...]
    q = jnp.dot(cq_ref[...], wq_ref[...], preferred_element_type=F32)
    kn = jnp.dot(ckv_ref[...], wk_ref[...], preferred_element_type=F32)
    kr = kr_ref[...]
    for hd in range(MLA_HEADS):
        lo = hd * MLA_QK_PAD
        q_ref[:, lo:lo + LANES] = (q[:, lo:lo + LANES] * MLA_QSCALE).astype(BF16)
        q_ref[:, lo + LANES:lo + 2 * LANES] = (_rope(
            q[:, lo + LANES:lo + 2 * LANES], c, s1, s2, MLA_ROPE // 2) * MLA_QSCALE).astype(BF16)
        k_ref[:, lo:lo + LANES] = kn[:, hd * LANES:(hd + 1) * LANES].astype(BF16)
        k_ref[:, lo + LANES:lo + 2 * LANES] = kr
    v_ref[...] = jnp.dot(ckv_ref[...], wv_ref[...], preferred_element_type=F32).astype(BF16)


def _mla_up(cq, ckv, kr, wq, wk, wv, tabs, tm=512):
    s = cq.shape[0]
    tm = min(tm, s)
    c, s1, s2 = tabs
    nq = MLA_HEADS * MLA_QK_PAD
    nv = MLA_HEADS * MLA_V
    tab_spec = pl.BlockSpec((tm, LANES), lambda i: (i, 0))
    return pl.pallas_call(
        _mla_up_kernel,
        grid=(s // tm,),
        in_specs=[pl.BlockSpec((tm, MLA_Q_LORA), lambda i: (i, 0)),
                  pl.BlockSpec((tm, MLA_KV_LORA), lambda i: (i, 0)),
                  pl.BlockSpec((tm, LANES), lambda i: (i, 0)),
                  pl.BlockSpec(wq.shape, lambda i: (0, 0)),
                  pl.BlockSpec(wk.shape, lambda i: (0, 0)),
                  pl.BlockSpec(wv.shape, lambda i: (0, 0)),
                  tab_spec, tab_spec, tab_spec],
        out_specs=[pl.BlockSpec((tm, nq), lambda i: (i, 0)),
                   pl.BlockSpec((tm, nq), lambda i: (i, 0)),
                   pl.BlockSpec((tm, nv), lambda i: (i, 0))],
        out_shape=[jax.ShapeDtypeStruct((s, nq), BF16),
                   jax.ShapeDtypeStruct((s, nq), BF16),
                   jax.ShapeDtypeStruct((s, nv), BF16)],
        compiler_params=_cparams(1),
        name="mla_up",
    )(cq, ckv, kr, wq, wk, wv, c, s1, s2)


def _flash_kernel(*refs, mode, tq, tk, lam_init):
    if mode == "diff":
        (q_ref, k_ref, vt_ref, lam_ref, subln_ref, o_ref,
         qcat_ref, s0_ref, s1_ref, p_ref, acc_ref) = refs
    else:
        (q_ref, k_ref, vt_ref, o_ref,
         qcat_ref, s0_ref, s1_ref, p_ref, acc_ref) = refs
    qt = q_ref[...].astype(F32).T
    if mode == "single":
        qcat_ref[...] = qt.astype(BF16)
    elif mode == "pair":
        qcat_ref[:, :tq] = qt[:LANES].astype(BF16)
        qcat_ref[:, tq:] = qt[LANES:].astype(BF16)
    else:
        row = lax.broadcasted_iota(jnp.int32, qt.shape, 0)
        qcat_ref[:, :tq] = jnp.where(row < DIFF_QK, qt, 0.0).astype(BF16)
        qcat_ref[:, tq:] = jnp.where(row >= DIFF_QK, qt, 0.0).astype(BF16)
    ncols = qcat_ref.shape[1]
    dv = vt_ref.shape[1]
    nk = vt_ref.shape[0]
    s_bufs = (s0_ref, s1_ref)
    n_pieces = tk // FLASH_PIECE

    def scores(ci, r, dst_ref):
        k0 = pl.multiple_of(ci * tk + r * FLASH_PIECE, FLASH_PIECE)
        sc = jnp.dot(k_ref[pl.ds(k0, FLASH_PIECE), :], qcat_ref[...],
                     preferred_element_type=F32)
        dst_ref[r * FLASH_PIECE:(r + 1) * FLASH_PIECE, :] = sc
        return jnp.max(sc, axis=0, keepdims=True)

    def probs(s_ref, r, mb, psum):
        for rr in range(FLASH_PIECE // BF16_ROWS):
            lo = r * FLASH_PIECE + rr * BF16_ROWS
            p0 = jnp.exp2(s_ref[lo:lo + SUBLANES, :] - mb)
            p1 = jnp.exp2(s_ref[lo + SUBLANES:lo + BF16_ROWS, :] - mb)
            psum = psum + p0 + p1
            p_ref[lo:lo + BF16_ROWS, :] = jnp.concatenate([p0, p1], axis=0).astype(BF16)
        return psum

    def pv_piece(ci, r):
        lo = r * FLASH_PIECE
        return jnp.dot(vt_ref.at[ci][:, lo:lo + FLASH_PIECE], p_ref[lo:lo + FLASH_PIECE, :],
                       preferred_element_type=F32)

    def body(j, carry):
        m, l, cmax = carry
        for b in range(2):
            ci = 2 * j + b
            nxt = jnp.minimum(ci + 1, nk - 1)
            m_new = jnp.maximum(m, cmax)
            alpha = jnp.exp2(m - m_new)
            mb = jnp.broadcast_to(m_new, (SUBLANES, ncols))
            psum = jnp.zeros((SUBLANES, ncols), F32)
            pv = None
            for r in range(n_pieces):
                cm = scores(nxt, r, s_bufs[1 - b])
                cmax_next = cm if r == 0 else jnp.maximum(cmax_next, cm)
                psum = probs(s_bufs[b], r, mb, psum)
                d = pv_piece(ci, r)
                pv = d if pv is None else pv + d
            acc_ref[...] = alpha * acc_ref[...] + pv
            l = alpha * l + jnp.sum(psum, axis=0, keepdims=True)
            m, cmax = m_new, cmax_next
        return m, l, cmax

    cmax0 = scores(0, 0, s_bufs[0])
    for r in range(1, n_pieces):
        cmax0 = jnp.maximum(cmax0, scores(0, r, s_bufs[0]))
    acc_ref[...] = jnp.zeros_like(acc_ref)
    init = (jnp.full((1, ncols), NEG_BIG, F32), jnp.zeros((1, ncols), F32), cmax0)
    _, l, _ = lax.fori_loop(0, nk // 2, body, init)
    ot = acc_ref[...] * (1.0 / l)
    if mode == "single":
        o_ref[...] = ot.T.astype(BF16)
    elif mode == "pair":
        o_ref[:, :dv] = ot[:, :tq].T.astype(BF16)
        o_ref[:, dv:] = ot[:, tq:].T.astype(BF16)
    else:
        lp = lam_ref[...]
        lam = (jnp.exp(jnp.sum(lp[0:1] * lp[1:2], axis=1, keepdims=True))
               - jnp.exp(jnp.sum(lp[2:3] * lp[3:4], axis=1, keepdims=True)) + lam_init)
        d = (ot[:, :tq] - lam * ot[:, tq:]).T
        o_ref[...] = (_rms(d, subln_ref[...]) * (1.0 - lam_init)).astype(BF16)


def _chunked_vt(v, heads, tk):
    s = v.shape[0]
    dv = v.shape[1] // heads
    return jnp.transpose(v.reshape(s // tk, tk, heads, dv), (2, 0, 3, 1))


def _flash(q_arr, k_arr, vt, *, mode, n_steps, q_blk, q_off, k_blk, k_off, out_w,
           tq=512, lam=None, subln=None, lam_init=0.0):
    s = q_arr.shape[0]
    tq = min(tq, s)
    _, nk, dv, tk = vt.shape
    assert nk % 2 == 0
    ncols = tq if mode == "single" else 2 * tq
    in_specs = [pl.BlockSpec((tq, q_blk), lambda h, i: (i, h + q_off)),
                pl.BlockSpec((s, k_blk), lambda h, i: (0, h + k_off)),
                pl.BlockSpec((None, nk, dv, tk), lambda h, i: (h, 0, 0, 0))]
    args = [q_arr, k_arr, vt]
    if mode == "diff":
        in_specs += [pl.BlockSpec(lam.shape, lambda h, i: (0, 0)),
                     pl.BlockSpec((1, HEAD_DIM), lambda h, i: (0, 0))]
        args += [lam, subln.reshape(1, HEAD_DIM)]
    return pl.pallas_call(
        functools.partial(_flash_kernel, mode=mode, tq=tq, tk=tk, lam_init=lam_init),
        grid=(n_steps, s // tq),
        in_specs=in_specs,
        out_specs=pl.BlockSpec((tq, out_w), lambda h, i: (i, h)),
        out_shape=jax.ShapeDtypeStruct((s, n_steps * out_w), BF16),
        scratch_shapes=[pltpu.VMEM((k_blk, ncols), BF16),
                        pltpu.VMEM((tk, ncols), F32), pltpu.VMEM((tk, ncols), F32),
                        pltpu.VMEM((tk, ncols), BF16),
                        pltpu.VMEM((dv, ncols), F32)],
        compiler_params=_cparams(2),
        name="flash_" + mode,
    )(*args)


def _swa_kernel(sink_ref, q_ref, k_ref, v_ref, o_ref, *, tq, win, scale):
    g = pl.program_id(0)
    i = pl.program_id(1)
    s = k_ref.shape[0]
    q0 = i * tq
    k0 = pl.multiple_of(jnp.clip(q0 - SWA_WINDOW, 0, s - win), LANES)
    kw = k_ref[pl.ds(k0, win), :]
    vw = v_ref[pl.ds(k0, win), :]
    qpos = q0 + lax.broadcasted_iota(jnp.int32, (tq, win), 0)
    kpos = k0 + lax.broadcasted_iota(jnp.int32, (tq, win), 1)
    valid = jnp.abs(kpos - qpos) <= SWA_WINDOW
    group = SWA_HEADS // SWA_KV_HEADS
    for j in range(group):
        q = q_ref[:, j * HEAD_DIM:(j + 1) * HEAD_DIM]
        sc = lax.dot_general(q, kw, (((1,), (1,)), ((), ())),
                             preferred_element_type=F32) * scale
        sc = jnp.where(valid, sc, NEG_BIG)
        sink = sink_ref[g * group + j]
        m = jnp.maximum(jnp.max(sc, axis=1, keepdims=True), sink)
        p = jnp.exp(sc - m)
        denom = jnp.sum(p, axis=1, keepdims=True) + jnp.exp(sink - m)
        pr = (p * (1.0 / denom)).astype(BF16)
        o_ref[:, j * HEAD_DIM:(j + 1) * HEAD_DIM] = jnp.dot(
            pr, vw, preferred_element_type=F32).astype(BF16)


def _swa(z, sink, tq=512):
    s = z.shape[0]
    tq = min(tq, s)
    win = min(tq + 2 * SWA_WINDOW, s)
    group = SWA_HEADS // SWA_KV_HEADS
    qw = group * HEAD_DIM
    k_off = SWA_HEADS
    v_off = SWA_HEADS + SWA_KV_HEADS
    return pl.pallas_call(
        functools.partial(_swa_kernel, tq=tq, win=win, scale=HEAD_DIM ** -0.5),
        grid=(SWA_KV_HEADS, s // tq),
        in_specs=[pl.BlockSpec(memory_space=pltpu.SMEM),
                  pl.BlockSpec((tq, qw), lambda g, i: (i, g)),
                  pl.BlockSpec((s, HEAD_DIM), lambda g, i: (0, g + k_off)),
                  pl.BlockSpec((s, HEAD_DIM), lambda g, i: (0, g + v_off))],
        out_specs=pl.BlockSpec((tq, qw), lambda g, i: (i, g)),
        out_shape=jax.ShapeDtypeStruct((s, SWA_HEADS * HEAD_DIM), BF16),
        compiler_params=_cparams(2),
        name="swa",
    )(sink, z, z, z)


def _merge_kernel(h_ref, b0_ref, b1_ref, b2_ref, b3_ref, wg_ref, wb_ref, o_ref):
    h = h_ref[...]
    acc = None
    for j, b_ref in enumerate((b0_ref, b1_ref, b2_ref, b3_ref)):
        gate = _sigmoid(jnp.dot(h, wg_ref[j], preferred_element_type=F32))
        term = gate * jnp.dot(b_ref[...], wb_ref[j], preferred_element_type=F32)
        acc = term if acc is None else acc + term
    o_ref[...] = acc.astype(BF16)


def _merge(h, branches, wg, wb, tm=512, tn=512):
    s, d = h.shape
    n = wg.shape[2]
    bw = wb.shape[1]
    tm = min(tm, s)
    b_spec = pl.BlockSpec((tm, bw), lambda j, i: (i, 0))
    return pl.pallas_call(
        _merge_kernel,
        grid=(n // tn, s // tm),
        in_specs=[pl.BlockSpec((tm, d), lambda j, i: (i, 0)),
                  b_spec, b_spec, b_spec, b_spec,
                  pl.BlockSpec((N_BRANCH, d, tn), lambda j, i: (0, 0, j)),
                  pl.BlockSpec((N_BRANCH, bw, tn), lambda j, i: (0, 0, j))],
        out_specs=pl.BlockSpec((tm, tn), lambda j, i: (i, j)),
        out_shape=jax.ShapeDtypeStruct((s, n), BF16),
        compiler_params=_cparams(2),
        name="merge",
    )(h, *branches, wg, wb)


def _attn_out_kernel(m_ref, w_ref, x_ref, gpost_ref, gnext_ref, xo_ref, ho_ref):
    y = jnp.dot(m_ref[...], w_ref[...], preferred_element_type=F32)
    xn = x_ref[...] + _rms(y, gpost_ref[...])
    xo_ref[...] = xn
    ho_ref[...] = _rms(xn, gnext_ref[...]).astype(BF16)


def _attn_out(merged, w, x, gpost, gnext, tm=512):
    s, d = x.shape
    tm = min(tm, s)
    row = pl.BlockSpec((tm, d), lambda i: (i, 0))
    vec = pl.BlockSpec((1, d), lambda i: (0, 0))
    return pl.pallas_call(
        _attn_out_kernel,
        grid=(s // tm,),
        in_specs=[row, _resident((d, d)), row, vec, vec],
        out_specs=[row, row],
        out_shape=[jax.ShapeDtypeStruct((s, d), F32), jax.ShapeDtypeStruct((s, d), BF16)],
        compiler_params=_cparams(1),
        name="attn_out",
    )(merged, w, x, gpost.reshape(1, d), gnext.reshape(1, d))


def _ffn_in_kernel(h_ref, wg_ref, wu_ref, o_ref):
    h = h_ref[...]
    gate = jnp.dot(h, wg_ref[...], preferred_element_type=F32)
    up = jnp.dot(h, wu_ref[...], preferred_element_type=F32)
    o_ref[...] = (gate * _sigmoid(gate) * up).astype(BF16)


def _ffn_in(h, w, tm=1024, tn=512):
    s, d = h.shape
    f = w.shape[1] // 2
    tm = min(tm, s)
    nt = f // tn
    return pl.pallas_call(
        _ffn_in_kernel,
        grid=(nt, s // tm),
        in_specs=[pl.BlockSpec((tm, d), lambda j, i: (i, 0)),
                  pl.BlockSpec((d, tn), lambda j, i: (0, j)),
                  pl.BlockSpec((d, tn), lambda j, i: (0, j + nt))],
        out_specs=pl.BlockSpec((tm, tn), lambda j, i: (i, j)),
        out_shape=jax.ShapeDtypeStruct((s, f), BF16),
        compiler_params=_cparams(2),
        name="ffn_in",
    )(h, w, w)


def _ffn_out_kernel(u_ref, w_ref, x_ref, g_ref, xo_ref, xb_ref):
    y = jnp.dot(u_ref[...], w_ref[...], preferred_element_type=F32)
    xn = x_ref[...] + _rms(y, g_ref[...])
    xo_ref[...] = xn
    xb_ref[...] = xn.astype(BF16)


def _ffn_out(u, w, x, g, tm=256):
    s, d = x.shape
    f = u.shape[1]
    tm = min(tm, s)
    row = pl.BlockSpec((tm, d), lambda i: (i, 0))
    return pl.pallas_call(
        _ffn_out_kernel,
        grid=(s // tm,),
        in_specs=[pl.BlockSpec((tm, f), lambda i: (i, 0)),
                  _resident((f, d)),
                  row, pl.BlockSpec((1, d), lambda i: (0, 0))],
        out_specs=[row, row],
        out_shape=[jax.ShapeDtypeStruct((s, d), F32), jax.ShapeDtypeStruct((s, d), BF16)],
        compiler_params=_cparams(1),
        name="ffn_out",
    )(u, w, x, g.reshape(1, d))


def _ple_kernel(xb_ref, wg_ref, p_ref, wp_ref, x_ref, gpost_ref, *rest):
    gate = _sigmoid(jnp.dot(xb_ref[...], wg_ref[...], preferred_element_type=F32))
    emb = jnp.dot(p_ref[...].astype(BF16), wp_ref[...], preferred_element_type=F32)
    xn = x_ref[...] + _rms(gate * emb, gpost_ref[...])
    if len(rest) == 3:
        gnext_ref, xo_ref, ho_ref = rest
        ho_ref[...] = _rms(xn, gnext_ref[...]).astype(BF16)
    else:
        (xo_ref,) = rest
    xo_ref[...] = xn


def _ple(xb, wg, p, wp, x, gpost, gnext, tm=512):
    s, d = x.shape
    pd = p.shape[1]
    tm = min(tm, s)
    row = pl.BlockSpec((tm, d), lambda i: (i, 0))
    vec = pl.BlockSpec((1, d), lambda i: (0, 0))
    in_specs = [row, _resident((d, d)), pl.BlockSpec((tm, pd), lambda i: (i, 0)),
                _resident((pd, d)), row, vec]
    args = [xb, wg, p, wp, x, gpost.reshape(1, d)]
    out_specs = [row]
    out_shape = [jax.ShapeDtypeStruct((s, d), F32)]
    if gnext is not None:
        in_specs.append(vec)
        args.append(gnext.reshape(1, d))
        out_specs.append(row)
        out_shape.append(jax.ShapeDtypeStruct((s, d), BF16))
    return pl.pallas_call(
        _ple_kernel,
        grid=(s // tm,),
        in_specs=in_specs,
        out_specs=out_specs,
        out_shape=out_shape,
        compiler_params=_cparams(1),
        name="ple",
    )(*args)


def _pad_cols(w, to):
    return jnp.pad(w, ((0, 0), (0, to - w.shape[1])))


def kernel(x, p, norm_mix_pre, norm_mix_post, norm_ffn_pre, norm_ffn_post, norm_ple_post, w_in, mla_qa_norm, mla_w_uq, mla_kva_norm, mla_w_ukv, diff_lambda, diff_subln, swa_sink, ax_q_norm, ax_k_norm, w_branch, w_branch_gate, w_o, w_ffn_in, w_ffn_out, w_ple, w_ple_gate):
    b, s, d = x.shape
    assert b == 1
    depth = w_in.shape[0]
    x2 = x.reshape(s, d)
    tab64, tab128, tab_ax = _position_tables(s)
    tk = min(FLASH_TK, s)

    mla_cols = MLA_Q_LORA + MLA_KV_LORA + MLA_ROPE
    diff_cols = 3 * DIFF_HEADS * HEAD_DIM
    swa_cols = (SWA_HEADS + 2 * SWA_KV_HEADS) * HEAD_DIM
    c1 = mla_cols
    c2 = c1 + diff_cols
    c3 = c2 + swa_cols

    diff_plan = tuple([("rope", 0, DIFF_QK ** -0.5 * LOG2E)] * 4 + [("rope", 0, 1.0)] * 4
                      + [("none", 0, 1.0)] * 4)
    swa_plan = tuple([("rope", 0, 1.0)] * 6 + [("none", 0, 1.0)] * 2)
    ax_plan = tuple([("normrope", 0, HEAD_DIM ** -0.5 * LOG2E)] * 4 + [("normrope", 1, 1.0)] * 2
                    + [("none", 0, 1.0)] * 2)
    no_gain = jnp.ones((1, LANES), F32)

    h = _prenorm(x2, norm_mix_pre[0])
    for i in range(depth):
        lam_init = 0.8 - 0.6 * math.exp(-0.3 * i)
        wi = w_in[i].astype(BF16)
        w_mla = _pad_cols(wi[:, :c1], MLA_Q_LORA + MLA_KV_LORA + LANES)
        w_diff, w_swa, w_ax = wi[:, c1:c2], wi[:, c2:c3], wi[:, c3:]

        cq, ckv, kr = _mla_in(h, w_mla, tab64, mla_qa_norm[i], mla_kva_norm[i])
        wq = jnp.pad(mla_w_uq[i].reshape(MLA_Q_LORA, MLA_HEADS, MLA_NOPE + MLA_ROPE),
                     ((0, 0), (0, 0), (0, MLA_QK_PAD - MLA_NOPE - MLA_ROPE))
                     ).reshape(MLA_Q_LORA, MLA_HEADS * MLA_QK_PAD).astype(BF16)
        wkv = mla_w_ukv[i].reshape(MLA_KV_LORA, MLA_HEADS, MLA_NOPE + MLA_V)
        wk = wkv[:, :, :MLA_NOPE].reshape(MLA_KV_LORA, MLA_HEADS * MLA_NOPE).astype(BF16)
        wv = wkv[:, :, MLA_NOPE:].reshape(MLA_KV_LORA, MLA_HEADS * MLA_V).astype(BF16)
        qa, ka, va = _mla_up(cq, ckv, kr, wq, wk, wv, tab64)
        br_a = _flash(qa, ka, _chunked_vt(va, MLA_HEADS, tk), mode="single",
                      n_steps=MLA_HEADS, q_blk=MLA_QK_PAD, q_off=0, k_blk=MLA_QK_PAD, k_off=0,
                      out_w=MLA_V, tq=1024)

        zb = _inproj(h, w_diff, tab64, no_gain, diff_plan, DIFF_QK // 2)
        vb = zb[:, 2 * DIFF_HEADS * HEAD_DIM:]
        br_b = _flash(zb, zb, _chunked_vt(vb, DIFF_HEADS, tk), mode="diff",
                      n_steps=DIFF_HEADS, q_blk=HEAD_DIM, q_off=0, k_blk=HEAD_DIM,
                      k_off=DIFF_HEADS, out_w=HEAD_DIM,
                      lam=diff_lambda[i], subln=diff_subln[i], lam_init=lam_init)

        zc = _inproj(h, w_swa, tab128, no_gain, swa_plan, HEAD_DIM // 2)
        br_c = _swa(zc, swa_sink[i])

        gains_ax = jnp.stack([ax_q_norm[i], ax_k_norm[i]])
        zd = _inproj(h, w_ax, tab_ax, gains_ax, ax_plan, HEAD_DIM // 4)
        vd = zd[:, (AX_HEADS + AX_KV_HEADS) * HEAD_DIM:]
        group = AX_HEADS // AX_KV_HEADS
        br_d = _flash(zd, zd, _chunked_vt(vd, AX_KV_HEADS, tk), mode="pair",
                      n_steps=AX_KV_HEADS, q_blk=group * HEAD_DIM, q_off=0, k_blk=HEAD_DIM,
                      k_off=AX_HEADS, out_w=group * HEAD_DIM)

        merged = _merge(h, (br_a, br_b, br_c, br_d),
                        w_branch_gate[i].astype(BF16), w_branch[i].astype(BF16))
        x2, h2 = _attn_out(merged, w_o[i].astype(BF16), x2, norm_mix_post[i], norm_ffn_pre[i])

        u = _ffn_in(h2, w_ffn_in[i].astype(BF16))
        x2, xb = _ffn_out(u, w_ffn_out[i].astype(BF16), x2, norm_ffn_post[i])

        g_next = norm_mix_pre[i + 1] if i + 1 < depth else None
        outs = _ple(xb, w_ple_gate[i].astype(BF16), p[i, 0], w_ple[i].astype(BF16), x2,
                    norm_ple_post[i], g_next)
        x2 = outs[0]
        h = outs[1] if g_next is not None else None
    return x2.reshape(b, s, d)
```

```python
import functools
import math

import jax
import jax.numpy as jnp
from jax import lax
from jax.experimental import pallas as pl
from jax.experimental.pallas import tpu as pltpu

HEAD_DIM = 128
ROPE_THETA = 10000.0
RMS_EPS = 1e-6
GRID_W = 64
MLA_HEADS = 4
MLA_Q_LORA = 512
MLA_KV_LORA = 512
MLA_NOPE = 128
MLA_ROPE = 64
MLA_V = 128
DIFF_HEADS = 4
DIFF_QK = HEAD_DIM // 2
SWA_HEADS = 4
SWA_KV_HEADS = 2
SWA_WINDOW = 128
AX_HEADS = 4
AX_KV_HEADS = 2
N_BRANCH = 4

LANES = 128
SUBLANES = 8
BF16_ROWS = 16
MLA_QK_PAD = 2 * LANES
LOG2E = 1.4426950408889634
NEG_BIG = -1e30
MLA_QSCALE = (MLA_NOPE + MLA_ROPE) ** -0.5 * LOG2E
FLASH_TK = 1024
FLASH_PIECE = 256
FLASH_UNROLL = 8
VMEM_LIMIT = 56 * 1024 * 1024

BF16 = jnp.bfloat16
F32 = jnp.float32


def _cparams(n_grid):
    return pltpu.CompilerParams(
        dimension_semantics=("arbitrary",) * n_grid,
        vmem_limit_bytes=VMEM_LIMIT,
    )


def _resident(shape):
    return pl.BlockSpec(shape, lambda *_: (0,) * len(shape), pipeline_mode=pl.Buffered(1))


def _rms(y, g):
    return y * lax.rsqrt(jnp.mean(y * y, axis=-1, keepdims=True) + RMS_EPS) * g


def _rope(x, c, s1, s2, half):
    return (x * c + pltpu.roll(x, LANES - half, 1) * s1
            + pltpu.roll(x, half, 1) * s2)


def _sigmoid(x):
    return 1.0 / (1.0 + jnp.exp(-x))


def _position_tables(s):
    t = jnp.arange(s, dtype=F32)
    rows = s // GRID_W
    row_pos = jnp.broadcast_to(jnp.arange(rows)[:, None], (rows, GRID_W)).reshape(s).astype(F32)
    col_pos = jnp.broadcast_to(jnp.arange(GRID_W)[None, :], (rows, GRID_W)).reshape(s).astype(F32)
    inv64 = ROPE_THETA ** (-jnp.arange(0, 64, 2, dtype=F32) / 64)
    inv128 = ROPE_THETA ** (-jnp.arange(0, 128, 2, dtype=F32) / 128)
    lane = jnp.arange(LANES)
    first32 = (lane % 64) < 32
    first64 = lane < 64

    def tables(ang, first):
        c, sn = jnp.cos(ang), jnp.sin(ang)
        return (c, jnp.where(first[None, :], -sn, 0.0), jnp.where(first[None, :], 0.0, sn))

    ang64 = t[:, None] * inv64[lane % 32][None, :]
    ang128 = t[:, None] * inv128[lane % 64][None, :]
    pos_ax = jnp.where(first64[None, :], row_pos[:, None], col_pos[:, None])
    ang_ax = pos_ax * inv64[lane % 32][None, :]
    return tables(ang64, first32), tables(ang128, first64), tables(ang_ax, first32)


def _prenorm_kernel(x_ref, g_ref, o_ref):
    o_ref[...] = _rms(x_ref[...], g_ref[...]).astype(BF16)


def _prenorm(x, g, tm=512):
    s, d = x.shape
    tm = min(tm, s)
    return pl.pallas_call(
        _prenorm_kernel,
        grid=(s // tm,),
        in_specs=[pl.BlockSpec((tm, d), lambda i: (i, 0)),
                  pl.BlockSpec((1, d), lambda i: (0, 0))],
        out_specs=pl.BlockSpec((tm, d), lambda i: (i, 0)),
        out_shape=jax.ShapeDtypeStruct((s, d), BF16),
        compiler_params=_cparams(1),
        name="prenorm",
    )(x, g.reshape(1, d))


def _inproj_kernel(h_ref, w_ref, c_ref, s1_ref, s2_ref, g_ref, o_ref, *, plan, half):
    acc = jnp.dot(h_ref[...], w_ref[...], preferred_element_type=F32)
    c, s1, s2 = c_ref[...], s1_ref[...], s2_ref[...]
    for b, (kind, gi, scale) in enumerate(plan):
        blk = acc[:, b * LANES:(b + 1) * LANES]
        if kind == "normrope":
            blk = _rms(blk, g_ref[gi:gi + 1, :])
        if kind in ("rope", "normrope"):
            blk = _rope(blk, c, s1, s2, half)
        if scale != 1.0:
            blk = blk * scale
        o_ref[:, b * LANES:(b + 1) * LANES] = blk.astype(BF16)


def _inproj(h, w, tabs, gains, plan, half, tm=512):
    s, d = h.shape
    n = w.shape[1]
    tm = min(tm, s)
    c, s1, s2 = tabs
    ng = gains.shape[0]
    tab_spec = pl.BlockSpec((tm, LANES), lambda i: (i, 0))
    return pl.pallas_call(
        functools.partial(_inproj_kernel, plan=plan, half=half),
        grid=(s // tm,),
        in_specs=[pl.BlockSpec((tm, d), lambda i: (i, 0)),
                  pl.BlockSpec((d, n), lambda i: (0, 0)),
                  tab_spec, tab_spec, tab_spec,
                  pl.BlockSpec((ng, LANES), lambda i: (0, 0))],
        out_specs=pl.BlockSpec((tm, n), lambda i: (i, 0)),
        out_shape=jax.ShapeDtypeStruct((s, n), BF16),
        compiler_params=_cparams(1),
        name="inproj",
    )(h, w, c, s1, s2, gains)


def _mla_in_kernel(h_ref, w_ref, c_ref, s1_ref, s2_ref, gq_ref, gkv_ref,
                   cq_ref, ckv_ref, kr_ref):
    acc = jnp.dot(h_ref[...], w_ref[...], preferred_element_type=F32)
    cq_ref[...] = _rms(acc[:, :MLA_Q_LORA], gq_ref[...]).astype(BF16)
    ckv_ref[...] = _rms(acc[:, MLA_Q_LORA:MLA_Q_LORA + MLA_KV_LORA], gkv_ref[...]).astype(BF16)
    kr = acc[:, MLA_Q_LORA + MLA_KV_LORA:]
    kr_ref[...] = _rope(kr, c_ref[...], s1_ref[...], s2_ref[...], MLA_ROPE // 2).astype(BF16)


def _mla_in(h, w, tabs, gq, gkv, tm=512):
    s, d = h.shape
    n = w.shape[1]
    tm = min(tm, s)
    c, s1, s2 = tabs
    tab_spec = pl.BlockSpec((tm, LANES), lambda i: (i, 0))
    return pl.pallas_call(
        _mla_in_kernel,
        grid=(s // tm,),
        in_specs=[pl.BlockSpec((tm, d), lambda i: (i, 0)),
                  pl.BlockSpec((d, n), lambda i: (0, 0)),
                  tab_spec, tab_spec, tab_spec,
                  pl.BlockSpec((1, MLA_Q_LORA), lambda i: (0, 0)),
                  pl.BlockSpec((1, MLA_KV_LORA), lambda i: (0, 0))],
        out_specs=[pl.BlockSpec((tm, MLA_Q_LORA), lambda i: (i, 0)),
                   pl.BlockSpec((tm, MLA_KV_LORA), lambda i: (i, 0)),
                   pl.BlockSpec((tm, LANES), lambda i: (i, 0))],
        out_shape=[jax.ShapeDtypeStruct((s, MLA_Q_LORA), BF16),
                   jax.ShapeDtypeStruct((s, MLA_KV_LORA), BF16),
                   jax.ShapeDtypeStruct((s, LANES), BF16)],
        compiler_params=_cparams(1),
        name="mla_in",
    )(h, w, c, s1, s2, gq.reshape(1, -1), gkv.reshape(1, -1))


def _mla_up_kernel(cq_ref, ckv_ref, kr_ref, wq_ref, wk_ref, wv_ref,
                   c_ref, s1_ref, s2_ref, q_ref, k_ref, v_ref):
    c, s1, s2 = c_ref[...], s1_ref[...], s2_ref[...]
    q = jnp.dot(cq_ref[...], wq_ref[...], preferred_element_type=F32)
    kn = jnp.dot(ckv_ref[...], wk_ref[...], preferred_element_type=F32)
    kr = kr_ref[...]
    for hd in range(MLA_HEADS):
        lo = hd * MLA_QK_PAD
        q_ref[:, lo:lo + LANES] = (q[:, lo:lo + LANES] * MLA_QSCALE).astype(BF16)
        q_ref[:, lo + LANES:lo + 2 * LANES] = (_rope(
            q[:, lo + LANES:lo + 2 * LANES], c, s1, s2, MLA_ROPE // 2) * MLA_QSCALE).astype(BF16)
        k_ref[:, lo:lo + LANES] = kn[:, hd * LANES:(hd + 1) * LANES].astype(BF16)
        k_ref[:, lo + LANES:lo + 2 * LANES] = kr
    v_ref[...] = jnp.dot(ckv_ref[...], wv_ref[...], preferred_element_type=F32).astype(BF16)


def _mla_up(cq, ckv, kr, wq, wk, wv, tabs, tm=512):
    s = cq.shape[0]
    tm = min(tm, s)
    c, s1, s2 = tabs
    nq = MLA_HEADS * MLA_QK_PAD
    nv = MLA_HEADS * MLA_V
    tab_spec = pl.BlockSpec((tm, LANES), lambda i: (i, 0))
    return pl.pallas_call(
        _mla_up_kernel,
        grid=(s // tm,),
        in_specs=[pl.BlockSpec((tm, MLA_Q_LORA), lambda i: (i, 0)),
                  pl.BlockSpec((tm, MLA_KV_LORA), lambda i: (i, 0)),
                  pl.BlockSpec((tm, LANES), lambda i: (i, 0)),
                  pl.BlockSpec(wq.shape, lambda i: (0, 0)),
                  pl.BlockSpec(wk.shape, lambda i: (0, 0)),
                  pl.BlockSpec(wv.shape, lambda i: (0, 0)),
                  tab_spec, tab_spec, tab_spec],
        out_specs=[pl.BlockSpec((tm, nq), lambda i: (i, 0)),
                   pl.BlockSpec((tm, nq), lambda i: (i, 0)),
                   pl.BlockSpec((tm, nv), lambda i: (i, 0))],
        out_shape=[jax.ShapeDtypeStruct((s, nq), BF16),
                   jax.ShapeDtypeStruct((s, nq), BF16),
                   jax.ShapeDtypeStruct((s, nv), BF16)],
        compiler_params=_cparams(1),
        name="mla_up",
    )(cq, ckv, kr, wq, wk, wv, c, s1, s2)


def _flash_kernel(*refs, mode, tq, tk, unroll, lam_init):
    if mode == "diff":
        (q_ref, k_ref, vt_ref, lam_ref, subln_ref, o_ref,
         qcat_ref, s0_ref, s1_ref, p_ref, acc_ref) = refs
    else:
        (q_ref, k_ref, vt_ref, o_ref,
         qcat_ref, s0_ref, s1_ref, p_ref, acc_ref) = refs
    qt = q_ref[...].astype(F32).T
    if mode == "single":
        qcat_ref[...] = qt.astype(BF16)
    elif mode == "pair":
        qcat_ref[:, :tq] = qt[:LANES].astype(BF16)
        qcat_ref[:, tq:] = qt[LANES:].astype(BF16)
    else:
        row = lax.broadcasted_iota(jnp.int32, qt.shape, 0)
        qcat_ref[:, :tq] = jnp.where(row < DIFF_QK, qt, 0.0).astype(BF16)
        qcat_ref[:, tq:] = jnp.where(row >= DIFF_QK, qt, 0.0).astype(BF16)
    ncols = qcat_ref.shape[1]
    dv = vt_ref.shape[1]
    nk = vt_ref.shape[0]
    s_bufs = (s0_ref, s1_ref)
    n_pieces = tk // FLASH_PIECE

    def scores(ci, r, dst_ref):
        k0 = pl.multiple_of(ci * tk + r * FLASH_PIECE, FLASH_PIECE)
        sc = jnp.dot(k_ref[pl.ds(k0, FLASH_PIECE), :], qcat_ref[...],
                     preferred_element_type=F32)
        dst_ref[r * FLASH_PIECE:(r + 1) * FLASH_PIECE, :] = sc
        return jnp.max(sc, axis=0, keepdims=True)

    def probs(s_ref, r, mb, psum):
        for rr in range(FLASH_PIECE // BF16_ROWS):
            lo = r * FLASH_PIECE + rr * BF16_ROWS
            p0 = jnp.exp2(s_ref[lo:lo + SUBLANES, :] - mb)
            p1 = jnp.exp2(s_ref[lo + SUBLANES:lo + BF16_ROWS, :] - mb)
            psum = psum + p0 + p1
            p_ref[lo:lo + BF16_ROWS, :] = jnp.concatenate([p0, p1], axis=0).astype(BF16)
        return psum

    def pv_piece(ci, r):
        lo = r * FLASH_PIECE
        return jnp.dot(vt_ref.at[ci][:, lo:lo + FLASH_PIECE], p_ref[lo:lo + FLASH_PIECE, :],
                       preferred_element_type=F32)

    def body(j, carry):
        m, l, cmax = carry
        for b in range(unroll):
            ci = unroll * j + b
            nxt = jnp.minimum(ci + 1, nk - 1)
            m_new = jnp.maximum(m, cmax)
            alpha = jnp.exp2(m - m_new)
            mb = jnp.broadcast_to(m_new, (SUBLANES, ncols))
            psum = jnp.zeros((SUBLANES, ncols), F32)
            pv = None
            for r in range(n_pieces):
                cm = scores(nxt, r, s_bufs[(b + 1) % 2])
                cmax_next = cm if r == 0 else jnp.maximum(cmax_next, cm)
                psum = probs(s_bufs[b % 2], r, mb, psum)
                d = pv_piece(ci, r)
                pv = d if pv is None else pv + d
            acc_ref[...] = alpha * acc_ref[...] + pv
            l = alpha * l + jnp.sum(psum, axis=0, keepdims=True)
            m, cmax = m_new, cmax_next
        return m, l, cmax

    cmax0 = scores(0, 0, s_bufs[0])
    for r in range(1, n_pieces):
        cmax0 = jnp.maximum(cmax0, scores(0, r, s_bufs[0]))
    acc_ref[...] = jnp.zeros_like(acc_ref)
    init = (jnp.full((1, ncols), NEG_BIG, F32), jnp.zeros((1, ncols), F32), cmax0)
    _, l, _ = lax.fori_loop(0, nk // unroll, body, init)
    ot = acc_ref[...] * (1.0 / l)
    if mode == "single":
        o_ref[...] = ot.T.astype(BF16)
    elif mode == "pair":
        o_ref[:, :dv] = ot[:, :tq].T.astype(BF16)
        o_ref[:, dv:] = ot[:, tq:].T.astype(BF16)
    else:
        lp = lam_ref[...]
        lam = (jnp.exp(jnp.sum(lp[0:1] * lp[1:2], axis=1, keepdims=True))
               - jnp.exp(jnp.sum(lp[2:3] * lp[3:4], axis=1, keepdims=True)) + lam_init)
        d = (ot[:, :tq] - lam * ot[:, tq:]).T
        o_ref[...] = (_rms(d, subln_ref[...]) * (1.0 - lam_init)).astype(BF16)


def _chunked_vt(v, heads, tk):
    s = v.shape[0]
    dv = v.shape[1] // heads
    return jnp.transpose(v.reshape(s // tk, tk, heads, dv), (2, 0, 3, 1))


def _flash(q_arr, k_arr, vt, *, mode, n_steps, q_blk, q_off, k_blk, k_off, out_w,
           tq=512, lam=None, subln=None, lam_init=0.0):
    s = q_arr.shape[0]
    tq = min(tq, s)
    _, nk, dv, tk = vt.shape
    unroll = math.gcd(nk, FLASH_UNROLL)
    assert unroll % 2 == 0
    ncols = tq if mode == "single" else 2 * tq
    in_specs = [pl.BlockSpec((tq, q_blk), lambda h, i: (i, h + q_off)),
                pl.BlockSpec((s, k_blk), lambda h, i: (0, h + k_off)),
                pl.BlockSpec((None, nk, dv, tk), lambda h, i: (h, 0, 0, 0))]
    args = [q_arr, k_arr, vt]
    if mode == "diff":
        in_specs += [pl.BlockSpec(lam.shape, lambda h, i: (0, 0)),
                     pl.BlockSpec((1, HEAD_DIM), lambda h, i: (0, 0))]
        args += [lam, subln.reshape(1, HEAD_DIM)]
    return pl.pallas_call(
        functools.partial(_flash_kernel, mode=mode, tq=tq, tk=tk, unroll=unroll,
                          lam_init=lam_init),
        grid=(n_steps, s // tq),
        in_specs=in_specs,
        out_specs=pl.BlockSpec((tq, out_w), lambda h, i: (i, h)),
        out_shape=jax.ShapeDtypeStruct((s, n_steps * out_w), BF16),
        scratch_shapes=[pltpu.VMEM((k_blk, ncols), BF16),
                        pltpu.VMEM((tk, ncols), F32), pltpu.VMEM((tk, ncols), F32),
                        pltpu.VMEM((tk, ncols), BF16),
                        pltpu.VMEM((dv, ncols), F32)],
        compiler_params=_cparams(2),
        name="flash_" + mode,
    )(*args)


def _swa_kernel(sink_ref, q_ref, k_ref, v_ref, o_ref, *, tq, win, scale):
    g = pl.program_id(0)
    i = pl.program_id(1)
    s = k_ref.shape[0]
    q0 = i * tq
    k0 = pl.multiple_of(jnp.clip(q0 - SWA_WINDOW, 0, s - win), LANES)
    kw = k_ref[pl.ds(k0, win), :]
    vw = v_ref[pl.ds(k0, win), :]
    qpos = q0 + lax.broadcasted_iota(jnp.int32, (tq, win), 0)
    kpos = k0 + lax.broadcasted_iota(jnp.int32, (tq, win), 1)
    valid = jnp.abs(kpos - qpos) <= SWA_WINDOW
    group = SWA_HEADS // SWA_KV_HEADS
    for j in range(group):
        q = q_ref[:, j * HEAD_DIM:(j + 1) * HEAD_DIM]
        sc = lax.dot_general(q, kw, (((1,), (1,)), ((), ())),
                             preferred_element_type=F32) * scale
        sc = jnp.where(valid, sc, NEG_BIG)
        sink = sink_ref[g * group + j]
        m = jnp.maximum(jnp.max(sc, axis=1, keepdims=True), sink)
        p = jnp.exp(sc - m)
        denom = jnp.sum(p, axis=1, keepdims=True) + jnp.exp(sink - m)
        pr = (p * (1.0 / denom)).astype(BF16)
        o_ref[:, j * HEAD_DIM:(j + 1) * HEAD_DIM] = jnp.dot(
            pr, vw, preferred_element_type=F32).astype(BF16)


def _swa(z, sink, tq=512):
    s = z.shape[0]
    tq = min(tq, s)
    win = min(tq + 2 * SWA_WINDOW, s)
    group = SWA_HEADS // SWA_KV_HEADS
    qw = group * HEAD_DIM
    k_off = SWA_HEADS
    v_off = SWA_HEADS + SWA_KV_HEADS
    return pl.pallas_call(
        functools.partial(_swa_kernel, tq=tq, win=win, scale=HEAD_DIM ** -0.5),
        grid=(SWA_KV_HEADS, s // tq),
        in_specs=[pl.BlockSpec(memory_space=pltpu.SMEM),
                  pl.BlockSpec((tq, qw), lambda g, i: (i, g)),
                  pl.BlockSpec((s, HEAD_DIM), lambda g, i: (0, g + k_off)),
                  pl.BlockSpec((s, HEAD_DIM), lambda g, i: (0, g + v_off))],
        out_specs=pl.BlockSpec((tq, qw), lambda g, i: (i, g)),
        out_shape=jax.ShapeDtypeStruct((s, SWA_HEADS * HEAD_DIM), BF16),
        compiler_params=_cparams(2),
        name="swa",
    )(sink, z, z, z)


def _merge_kernel(h_ref, b0_ref, b1_ref, b2_ref, b3_ref, wg_ref, wb_ref, o_ref):
    h = h_ref[...]
    acc = None
    for j, b_ref in enumerate((b0_ref, b1_ref, b2_ref, b3_ref)):
        gate = _sigmoid(jnp.dot(h, wg_ref[j], preferred_element_type=F32))
        term = gate * jnp.dot(b_ref[...], wb_ref[j], preferred_element_type=F32)
        acc = term if acc is None else acc + term
    o_ref[...] = acc.astype(BF16)


def _merge(h, branches, wg, wb, tm=512, tn=512):
    s, d = h.shape
    n = wg.shape[2]
    bw = wb.shape[1]
    tm = min(tm, s)
    b_spec = pl.BlockSpec((tm, bw), lambda j, i: (i, 0))
    return pl.pallas_call(
        _merge_kernel,
        grid=(n // tn, s // tm),
        in_specs=[pl.BlockSpec((tm, d), lambda j, i: (i, 0)),
                  b_spec, b_spec, b_spec, b_spec,
                  pl.BlockSpec((N_BRANCH, d, tn), lambda j, i: (0, 0, j)),
                  pl.BlockSpec((N_BRANCH, bw, tn), lambda j, i: (0, 0, j))],
        out_specs=pl.BlockSpec((tm, tn), lambda j, i: (i, j)),
        out_shape=jax.ShapeDtypeStruct((s, n), BF16),
        compiler_params=_cparams(2),
        name="merge",
    )(h, *branches, wg, wb)


def _attn_out_kernel(m_ref, w_ref, x_ref, gpost_ref, gnext_ref, xo_ref, ho_ref):
    y = jnp.dot(m_ref[...], w_ref[...], preferred_element_type=F32)
    xn = x_ref[...] + _rms(y, gpost_ref[...])
    xo_ref[...] = xn
    ho_ref[...] = _rms(xn, gnext_ref[...]).astype(BF16)


def _attn_out(merged, w, x, gpost, gnext, tm=512):
    s, d = x.shape
    tm = min(tm, s)
    row = pl.BlockSpec((tm, d), lambda i: (i, 0))
    vec = pl.BlockSpec((1, d), lambda i: (0, 0))
    return pl.pallas_call(
        _attn_out_kernel,
        grid=(s // tm,),
        in_specs=[row, _resident((d, d)), row, vec, vec],
        out_specs=[row, row],
        out_shape=[jax.ShapeDtypeStruct((s, d), F32), jax.ShapeDtypeStruct((s, d), BF16)],
        compiler_params=_cparams(1),
        name="attn_out",
    )(merged, w, x, gpost.reshape(1, d), gnext.reshape(1, d))


def _ffn_in_kernel(h_ref, wg_ref, wu_ref, o_ref):
    h = h_ref[...]
    gate = jnp.dot(h, wg_ref[...], preferred_element_type=F32)
    up = jnp.dot(h, wu_ref[...], preferred_element_type=F32)
    o_ref[...] = (gate * _sigmoid(gate) * up).astype(BF16)


def _ffn_in(h, w, tm=1024, tn=512):
    s, d = h.shape
    f = w.shape[1] // 2
    tm = min(tm, s)
    nt = f // tn
    return pl.pallas_call(
        _ffn_in_kernel,
        grid=(nt, s // tm),
        in_specs=[pl.BlockSpec((tm, d), lambda j, i: (i, 0)),
                  pl.BlockSpec((d, tn), lambda j, i: (0, j)),
                  pl.BlockSpec((d, tn), lambda j, i: (0, j + nt))],
        out_specs=pl.BlockSpec((tm, tn), lambda j, i: (i, j)),
        out_shape=jax.ShapeDtypeStruct((s, f), BF16),
        compiler_params=_cparams(2),
        name="ffn_in",
    )(h, w, w)


def _ffn_out_kernel(u_ref, w_ref, x_ref, g_ref, xo_ref, xb_ref):
    y = jnp.dot(u_ref[...], w_ref[...], preferred_element_type=F32)
    xn = x_ref[...] + _rms(y, g_ref[...])
    xo_ref[...] = xn
    xb_ref[...] = xn.astype(BF16)


def _ffn_out(u, w, x, g, tm=256):
    s, d = x.shape
    f = u.shape[1]
    tm = min(tm, s)
    row = pl.BlockSpec((tm, d), lambda i: (i, 0))
    return pl.pallas_call(
        _ffn_out_kernel,
        grid=(s // tm,),
        in_specs=[pl.BlockSpec((tm, f), lambda i: (i, 0)),
                  _resident((f, d)),
                  row, pl.BlockSpec((1, d), lambda i: (0, 0))],
        out_specs=[row, row],
        out_shape=[jax.ShapeDtypeStruct((s, d), F32), jax.ShapeDtypeStruct((s, d), BF16)],
        compiler_params=_cparams(1),
        name="ffn_out",
    )(u, w, x, g.reshape(1, d))


def _ple_kernel(xb_ref, wg_ref, p_ref, wp_ref, x_ref, gpost_ref, *rest):
    gate = _sigmoid(jnp.dot(xb_ref[...], wg_ref[...], preferred_element_type=F32))
    emb = jnp.dot(p_ref[...].astype(BF16), wp_ref[...], preferred_element_type=F32)
    xn = x_ref[...] + _rms(gate * emb, gpost_ref[...])
    if len(rest) == 3:
        gnext_ref, xo_ref, ho_ref = rest
        ho_ref[...] = _rms(xn, gnext_ref[...]).astype(BF16)
    else:
        (xo_ref,) = rest
    xo_ref[...] = xn


def _ple(xb, wg, p, wp, x, gpost, gnext, tm=512):
    s, d = x.shape
    pd = p.shape[1]
    tm = min(tm, s)
    row = pl.BlockSpec((tm, d), lambda i: (i, 0))
    vec = pl.BlockSpec((1, d), lambda i: (0, 0))
    in_specs = [row, _resident((d, d)), pl.BlockSpec((tm, pd), lambda i: (i, 0)),
                _resident((pd, d)), row, vec]
    args = [xb, wg, p, wp, x, gpost.reshape(1, d)]
    out_specs = [row]
    out_shape = [jax.ShapeDtypeStruct((s, d), F32)]
    if gnext is not None:
        in_specs.append(vec)
        args.append(gnext.reshape(1, d))
        out_specs.append(row)
        out_shape.append(jax.ShapeDtypeStruct((s, d), BF16))
    return pl.pallas_call(
        _ple_kernel,
        grid=(s // tm,),
        in_specs=in_specs,
        out_specs=out_specs,
        out_shape=out_shape,
        compiler_params=_cparams(1),
        name="ple",
    )(*args)


def _pad_cols(w, to):
    return jnp.pad(w, ((0, 0), (0, to - w.shape[1])))


def kernel(x, p, norm_mix_pre, norm_mix_post, norm_ffn_pre, norm_ffn_post, norm_ple_post, w_in, mla_qa_norm, mla_w_uq, mla_kva_norm, mla_w_ukv, diff_lambda, diff_subln, swa_sink, ax_q_norm, ax_k_norm, w_branch, w_branch_gate, w_o, w_ffn_in, w_ffn_out, w_ple, w_ple_gate):
    b, s, d = x.shape
    assert b == 1
    depth = w_in.shape[0]
    x2 = x.reshape(s, d)
    tab64, tab128, tab_ax = _position_tables(s)
    tk = min(FLASH_TK, s)

    mla_cols = MLA_Q_LORA + MLA_KV_LORA + MLA_ROPE
    diff_cols = 3 * DIFF_HEADS * HEAD_DIM
    swa_cols = (SWA_HEADS + 2 * SWA_KV_HEADS) * HEAD_DIM
    c1 = mla_cols
    c2 = c1 + diff_cols
    c3 = c2 + swa_cols

    diff_plan = tuple([("rope", 0, DIFF_QK ** -0.5 * LOG2E)] * 4 + [("rope", 0, 1.0)] * 4
                      + [("none", 0, 1.0)] * 4)
    swa_plan = tuple([("rope", 0, 1.0)] * 6 + [("none", 0, 1.0)] * 2)
    ax_plan = tuple([("normrope", 0, HEAD_DIM ** -0.5 * LOG2E)] * 4 + [("normrope", 1, 1.0)] * 2
                    + [("none", 0, 1.0)] * 2)
    no_gain = jnp.ones((1, LANES), F32)

    h = _prenorm(x2, norm_mix_pre[0])
    for i in range(depth):
        lam_init = 0.8 - 0.6 * math.exp(-0.3 * i)
        wi = w_in[i].astype(BF16)
        w_mla = _pad_cols(wi[:, :c1], MLA_Q_LORA + MLA_KV_LORA + LANES)
        w_diff, w_swa, w_ax = wi[:, c1:c2], wi[:, c2:c3], wi[:, c3:]

        cq, ckv, kr = _mla_in(h, w_mla, tab64, mla_qa_norm[i], mla_kva_norm[i])
        wq = jnp.pad(mla_w_uq[i].reshape(MLA_Q_LORA, MLA_HEADS, MLA_NOPE + MLA_ROPE),
                     ((0, 0), (0, 0), (0, MLA_QK_PAD - MLA_NOPE - MLA_ROPE))
                     ).reshape(MLA_Q_LORA, MLA_HEADS * MLA_QK_PAD).astype(BF16)
        wkv = mla_w_ukv[i].reshape(MLA_KV_LORA, MLA_HEADS, MLA_NOPE + MLA_V)
        wk = wkv[:, :, :MLA_NOPE].reshape(MLA_KV_LORA, MLA_HEADS * MLA_NOPE).astype(BF16)
        wv = wkv[:, :, MLA_NOPE:].reshape(MLA_KV_LORA, MLA_HEADS * MLA_V).astype(BF16)
        qa, ka, va = _mla_up(cq, ckv, kr, wq, wk, wv, tab64)
        br_a = _flash(qa, ka, _chunked_vt(va, MLA_HEADS, tk), mode="single",
                      n_steps=MLA_HEADS, q_blk=MLA_QK_PAD, q_off=0, k_blk=MLA_QK_PAD, k_off=0,
                      out_w=MLA_V, tq=1024)

        zb = _inproj(h, w_diff, tab64, no_gain, diff_plan, DIFF_QK // 2)
        vb = zb[:, 2 * DIFF_HEADS * HEAD_DIM:]
        br_b = _flash(zb, zb, _chunked_vt(vb, DIFF_HEADS, tk), mode="diff",
                      n_steps=DIFF_HEADS, q_blk=HEAD_DIM, q_off=0, k_blk=HEAD_DIM,
                      k_off=DIFF_HEADS, out_w=HEAD_DIM,
                      lam=diff_lambda[i], subln=diff_subln[i], lam_init=lam_init)

        zc = _inproj(h, w_swa, tab128, no_gain, swa_plan, HEAD_DIM // 2)
        br_c = _swa(zc, swa_sink[i])

        gains_ax = jnp.stack([ax_q_norm[i], ax_k_norm[i]])
        zd = _inproj(h, w_ax, tab_ax, gains_ax, ax_plan, HEAD_DIM // 4)
        vd = zd[:, (AX_HEADS + AX_KV_HEADS) * HEAD_DIM:]
        group = AX_HEADS // AX_KV_HEADS
        br_d = _flash(zd, zd, _chunked_vt(vd, AX_KV_HEADS, tk), mode="pair",
                      n_steps=AX_KV_HEADS, q_blk=group * HEAD_DIM, q_off=0, k_blk=HEAD_DIM,
                      k_off=AX_HEADS, out_w=group * HEAD_DIM)

        merged = _merge(h, (br_a, br_b, br_c, br_d),
                        w_branch_gate[i].astype(BF16), w_branch[i].astype(BF16))
        x2, h2 = _attn_out(merged, w_o[i].astype(BF16), x2, norm_mix_post[i], norm_ffn_pre[i])

        u = _ffn_in(h2, w_ffn_in[i].astype(BF16))
        x2, xb = _ffn_out(u, w_ffn_out[i].astype(BF16), x2, norm_ffn_post[i])

        g_next = norm_mix_pre[i + 1] if i + 1 < depth else None
        outs = _ple(xb, w_ple_gate[i].astype(BF16), p[i, 0], w_ple[i].astype(BF16), x2,
                    norm_ple_post[i], g_next)
        x2 = outs[0]
        h = outs[1] if g_next is not None else None
    return x2.reshape(b, s, d)
```

```python
import functools
import math

import jax
import jax.numpy as jnp
from jax import lax
from jax.experimental import pallas as pl
from jax.experimental.pallas import tpu as pltpu

HEAD_DIM = 128
ROPE_THETA = 10000.0
RMS_EPS = 1e-6
GRID_W = 64
MLA_HEADS = 4
MLA_Q_LORA = 512
MLA_KV_LORA = 512
MLA_NOPE = 128
MLA_ROPE = 64
MLA_V = 128
DIFF_HEADS = 4
DIFF_QK = HEAD_DIM // 2
SWA_HEADS = 4
SWA_KV_HEADS = 2
SWA_WINDOW = 128
AX_HEADS = 4
AX_KV_HEADS = 2
N_BRANCH = 4

LANES = 128
SUBLANES = 8
BF16_ROWS = 16
MLA_QK_PAD = 2 * LANES
LOG2E = 1.4426950408889634
NEG_BIG = -1e30
MLA_QSCALE = (MLA_NOPE + MLA_ROPE) ** -0.5 * LOG2E
FLASH_TK = 1024
FLASH_PIECE = 256
FLASH_UNROLL = 8
VMEM_LIMIT = 56 * 1024 * 1024

BF16 = jnp.bfloat16
F32 = jnp.float32


def _cparams(n_grid):
    return pltpu.CompilerParams(
        dimension_semantics=("arbitrary",) * n_grid,
        vmem_limit_bytes=VMEM_LIMIT,
    )


def _resident(shape):
    return pl.BlockSpec(shape, lambda *_: (0,) * len(shape), pipeline_mode=pl.Buffered(1))


def _rms(y, g):
    return y * lax.rsqrt(jnp.mean(y * y, axis=-1, keepdims=True) + RMS_EPS) * g


def _rope(x, c, s1, s2, half):
    return (x * c + pltpu.roll(x, LANES - half, 1) * s1
            + pltpu.roll(x, half, 1) * s2)


def _sigmoid(x):
    return 1.0 / (1.0 + jnp.exp(-x))


def _position_tables(s):
    t = jnp.arange(s, dtype=F32)
    rows = s // GRID_W
    row_pos = jnp.broadcast_to(jnp.arange(rows)[:, None], (rows, GRID_W)).reshape(s).astype(F32)
    col_pos = jnp.broadcast_to(jnp.arange(GRID_W)[None, :], (rows, GRID_W)).reshape(s).astype(F32)
    inv64 = ROPE_THETA ** (-jnp.arange(0, 64, 2, dtype=F32) / 64)
    inv128 = ROPE_THETA ** (-jnp.arange(0, 128, 2, dtype=F32) / 128)
    lane = jnp.arange(LANES)
    first32 = (lane % 64) < 32
    first64 = lane < 64

    def tables(ang, first):
        c, sn = jnp.cos(ang), jnp.sin(ang)
        return (c, jnp.where(first[None, :], -sn, 0.0), jnp.where(first[None, :], 0.0, sn))

    ang64 = t[:, None] * inv64[lane % 32][None, :]
    ang128 = t[:, None] * inv128[lane % 64][None, :]
    pos_ax = jnp.where(first64[None, :], row_pos[:, None], col_pos[:, None])
    ang_ax = pos_ax * inv64[lane % 32][None, :]
    return tables(ang64, first32), tables(ang128, first64), tables(ang_ax, first32)


def _prenorm_kernel(x_ref, g_ref, o_ref):
    o_ref[...] = _rms(x_ref[...], g_ref[...]).astype(BF16)


def _prenorm(x, g, tm=512):
    s, d = x.shape
    tm = min(tm, s)
    return pl.pallas_call(
        _prenorm_kernel,
        grid=(s // tm,),
        in_specs=[pl.BlockSpec((tm, d), lambda i: (i, 0)),
                  pl.BlockSpec((1, d), lambda i: (0, 0))],
        out_specs=pl.BlockSpec((tm, d), lambda i: (i, 0)),
        out_shape=jax.ShapeDtypeStruct((s, d), BF16),
        compiler_params=_cparams(1),
        name="prenorm",
    )(x, g.reshape(1, d))


def _vt_spec(heads, dv, tm, tk):
    r = tk // tm
    return pl.BlockSpec((heads, None, dv, tm), lambda i: (0, i // r, 0, i % r))


def _inproj_kernel(h_ref, w_ref, *refs, plan):
    tab_refs, g_ref = refs[:9], refs[9]
    outs = dict(zip(("zb", "vtb", "zc", "zd", "vtd"), refs[10:]))
    acc = jnp.dot(h_ref[...], w_ref[...], preferred_element_type=F32)
    for b, (tab, half, gi, scale, dest, db) in enumerate(plan):
        blk = acc[:, b * LANES:(b + 1) * LANES]
        if gi is not None:
            blk = _rms(blk, g_ref[gi:gi + 1, :])
        if tab is not None:
            c_ref, s1_ref, s2_ref = tab_refs[3 * tab:3 * tab + 3]
            blk = _rope(blk, c_ref[...], s1_ref[...], s2_ref[...], half)
        if scale != 1.0:
            blk = blk * scale
        if dest in ("vtb", "vtd"):
            outs[dest][db] = blk.T.astype(BF16)
        else:
            outs[dest][:, db * LANES:(db + 1) * LANES] = blk.astype(BF16)


def _inproj(h, w, tabs, gains, plan, tk, tm=512):
    s, d = h.shape
    n = w.shape[1]
    tm = min(tm, s)
    nk = s // tk
    tab_spec = pl.BlockSpec((tm, LANES), lambda i: (i, 0))
    nb = 2 * DIFF_HEADS * HEAD_DIM
    nc = (SWA_HEADS + 2 * SWA_KV_HEADS) * HEAD_DIM
    nd = (AX_HEADS + AX_KV_HEADS) * HEAD_DIM
    row = lambda width: pl.BlockSpec((tm, width), lambda i: (i, 0))
    return pl.pallas_call(
        functools.partial(_inproj_kernel, plan=plan),
        grid=(s // tm,),
        in_specs=[pl.BlockSpec((tm, d), lambda i: (i, 0)), _resident((d, n))]
                 + [tab_spec] * 9 + [_resident(gains.shape)],
        out_specs=[row(nb), _vt_spec(DIFF_HEADS, HEAD_DIM, tm, tk), row(nc),
                   row(nd), _vt_spec(AX_KV_HEADS, HEAD_DIM, tm, tk)],
        out_shape=[jax.ShapeDtypeStruct((s, nb), BF16),
                   jax.ShapeDtypeStruct((DIFF_HEADS, nk, HEAD_DIM, tk), BF16),
                   jax.ShapeDtypeStruct((s, nc), BF16),
                   jax.ShapeDtypeStruct((s, nd), BF16),
                   jax.ShapeDtypeStruct((AX_KV_HEADS, nk, HEAD_DIM, tk), BF16)],
        compiler_params=_cparams(1),
        name="inproj",
    )(h, w, *[t for tab in tabs for t in tab], gains)


def _mla_in_kernel(h_ref, w_ref, c_ref, s1_ref, s2_ref, gq_ref, gkv_ref,
                   cq_ref, ckv_ref, kr_ref):
    acc = jnp.dot(h_ref[...], w_ref[...], preferred_element_type=F32)
    cq_ref[...] = _rms(acc[:, :MLA_Q_LORA], gq_ref[...]).astype(BF16)
    ckv_ref[...] = _rms(acc[:, MLA_Q_LORA:MLA_Q_LORA + MLA_KV_LORA], gkv_ref[...]).astype(BF16)
    kr = acc[:, MLA_Q_LORA + MLA_KV_LORA:]
    kr_ref[...] = _rope(kr, c_ref[...], s1_ref[...], s2_ref[...], MLA_ROPE // 2).astype(BF16)


def _mla_in(h, w, tabs, gq, gkv, tm=512):
    s, d = h.shape
    n = w.shape[1]
    tm = min(tm, s)
    c, s1, s2 = tabs
    tab_spec = pl.BlockSpec((tm, LANES), lambda i: (i, 0))
    return pl.pallas_call(
        _mla_in_kernel,
        grid=(s // tm,),
        in_specs=[pl.BlockSpec((tm, d), lambda i: (i, 0)),
                  pl.BlockSpec((d, n), lambda i: (0, 0)),
                  tab_spec, tab_spec, tab_spec,
                  pl.BlockSpec((1, MLA_Q_LORA), lambda i: (0, 0)),
                  pl.BlockSpec((1, MLA_KV_LORA), lambda i: (0, 0))],
        out_specs=[pl.BlockSpec((tm, MLA_Q_LORA), lambda i: (i, 0)),
                   pl.BlockSpec((tm, MLA_KV_LORA), lambda i: (i, 0)),
                   pl.BlockSpec((tm, LANES), lambda i: (i, 0))],
        out_shape=[jax.ShapeDtypeStruct((s, MLA_Q_LORA), BF16),
                   jax.ShapeDtypeStruct((s, MLA_KV_LORA), BF16),
                   jax.ShapeDtypeStruct((s, LANES), BF16)],
        compiler_params=_cparams(1),
        name="mla_in",
    )(h, w, c, s1, s2, gq.reshape(1, -1), gkv.reshape(1, -1))


def _mla_up_kernel(cq_ref, ckv_ref, kr_ref, wq_ref, wk_ref, wv_ref,
                   c_ref, s1_ref, s2_ref, q_ref, k_ref, vt_ref):
    c, s1, s2 = c_ref[...], s1_ref[...], s2_ref[...]
    q = jnp.dot(cq_ref[...], wq_ref[...], preferred_element_type=F32)
    kn = jnp.dot(ckv_ref[...], wk_ref[...], preferred_element_type=F32)
    kr = kr_ref[...]
    for hd in range(MLA_HEADS):
        lo = hd * MLA_QK_PAD
        q_ref[:, lo:lo + LANES] = (q[:, lo:lo + LANES] * MLA_QSCALE).astype(BF16)
        q_ref[:, lo + LANES:lo + 2 * LANES] = (_rope(
            q[:, lo + LANES:lo + 2 * LANES], c, s1, s2, MLA_ROPE // 2) * MLA_QSCALE).astype(BF16)
        k_ref[:, lo:lo + LANES] = kn[:, hd * LANES:(hd + 1) * LANES].astype(BF16)
        k_ref[:, lo + LANES:lo + 2 * LANES] = kr
    v = jnp.dot(ckv_ref[...], wv_ref[...], preferred_element_type=F32)
    for hd in range(MLA_HEADS):
        vt_ref[hd] = v[:, hd * MLA_V:(hd + 1) * MLA_V].T.astype(BF16)


def _mla_up(cq, ckv, kr, wq, wk, wv, tabs, tk, tm=512):
    s = cq.shape[0]
    tm = min(tm, s)
    c, s1, s2 = tabs
    nq = MLA_HEADS * MLA_QK_PAD
    tab_spec = pl.BlockSpec((tm, LANES), lambda i: (i, 0))
    return pl.pallas_call(
        _mla_up_kernel,
        grid=(s // tm,),
        in_specs=[pl.BlockSpec((tm, MLA_Q_LORA), lambda i: (i, 0)),
                  pl.BlockSpec((tm, MLA_KV_LORA), lambda i: (i, 0)),
                  pl.BlockSpec((tm, LANES), lambda i: (i, 0)),
                  pl.BlockSpec(wq.shape, lambda i: (0, 0)),
                  pl.BlockSpec(wk.shape, lambda i: (0, 0)),
                  pl.BlockSpec(wv.shape, lambda i: (0, 0)),
                  tab_spec, tab_spec, tab_spec],
        out_specs=[pl.BlockSpec((tm, nq), lambda i: (i, 0)),
                   pl.BlockSpec((tm, nq), lambda i: (i, 0)),
                   _vt_spec(MLA_HEADS, MLA_V, tm, tk)],
        out_shape=[jax.ShapeDtypeStruct((s, nq), BF16),
                   jax.ShapeDtypeStruct((s, nq), BF16),
                   jax.ShapeDtypeStruct((MLA_HEADS, s // tk, MLA_V, tk), BF16)],
        compiler_params=_cparams(1),
        name="mla_up",
    )(cq, ckv, kr, wq, wk, wv, c, s1, s2)


def _flash_kernel(*refs, mode, tq, tk, unroll, lam_init):
    if mode == "diff":
        (q_ref, k_ref, vt_ref, lam_ref, subln_ref, o_ref,
         qcat_ref, s0_ref, s1_ref, p_ref, acc_ref) = refs
    else:
        (q_ref, k_ref, vt_ref, o_ref,
         qcat_ref, s0_ref, s1_ref, p_ref, acc_ref) = refs
    qt = q_ref[...].astype(F32).T
    if mode == "single":
        qcat_ref[...] = qt.astype(BF16)
    elif mode == "pair":
        qcat_ref[:, :tq] = qt[:LANES].astype(BF16)
        qcat_ref[:, tq:] = qt[LANES:].astype(BF16)
    else:
        row = lax.broadcasted_iota(jnp.int32, qt.shape, 0)
        qcat_ref[:, :tq] = jnp.where(row < DIFF_QK, qt, 0.0).astype(BF16)
        qcat_ref[:, tq:] = jnp.where(row >= DIFF_QK, qt, 0.0).astype(BF16)
    ncols = qcat_ref.shape[1]
    dv = vt_ref.shape[1]
    nk = vt_ref.shape[0]
    s_bufs = (s0_ref, s1_ref)
    n_pieces = tk // FLASH_PIECE

    def scores(ci, r, dst_ref):
        k0 = pl.multiple_of(ci * tk + r * FLASH_PIECE, FLASH_PIECE)
        sc = jnp.dot(k_ref[pl.ds(k0, FLASH_PIECE), :], qcat_ref[...],
                     preferred_element_type=F32)
        dst_ref[r * FLASH_PIECE:(r + 1) * FLASH_PIECE, :] = sc
        return jnp.max(sc, axis=0, keepdims=True)

    def probs(s_ref, r, mb, psum):
        for rr in range(FLASH_PIECE // BF16_ROWS):
            lo = r * FLASH_PIECE + rr * BF16_ROWS
            p0 = jnp.exp2(s_ref[lo:lo + SUBLANES, :] - mb)
            p1 = jnp.exp2(s_ref[lo + SUBLANES:lo + BF16_ROWS, :] - mb)
            psum = psum + p0 + p1
            p_ref[lo:lo + BF16_ROWS, :] = jnp.concatenate([p0, p1], axis=0).astype(BF16)
        return psum

    def pv_piece(ci, r):
        lo = r * FLASH_PIECE
        return jnp.dot(vt_ref.at[ci][:, lo:lo + FLASH_PIECE], p_ref[lo:lo + FLASH_PIECE, :],
                       preferred_element_type=F32)

    def body(j, carry):
        m, l, cmax = carry
        for b in range(unroll):
            ci = unroll * j + b
            nxt = jnp.minimum(ci + 1, nk - 1)
            m_new = jnp.maximum(m, cmax)
            alpha = jnp.exp2(m - m_new)
            mb = jnp.broadcast_to(m_new, (SUBLANES, ncols))
            psum = jnp.zeros((SUBLANES, ncols), F32)
            pv = None
            for r in range(n_pieces):
                cm = scores(nxt, r, s_bufs[(b + 1) % 2])
                cmax_next = cm if r == 0 else jnp.maximum(cmax_next, cm)
                psum = probs(s_bufs[b % 2], r, mb, psum)
                d = pv_piece(ci, r)
                pv = d if pv is None else pv + d
            acc_ref[...] = alpha * acc_ref[...] + pv
            l = alpha * l + jnp.sum(psum, axis=0, keepdims=True)
            m, cmax = m_new, cmax_next
        return m, l, cmax

    cmax0 = scores(0, 0, s_bufs[0])
    for r in range(1, n_pieces):
        cmax0 = jnp.maximum(cmax0, scores(0, r, s_bufs[0]))
    acc_ref[...] = jnp.zeros_like(acc_ref)
    init = (jnp.full((1, ncols), NEG_BIG, F32), jnp.zeros((1, ncols), F32), cmax0)
    _, l, _ = lax.fori_loop(0, nk // unroll, body, init)
    ot = acc_ref[...] * (1.0 / l)
    if mode == "single":
        o_ref[...] = ot.T.astype(BF16)
    elif mode == "pair":
        o_ref[:, :dv] = ot[:, :tq].T.astype(BF16)
        o_ref[:, dv:] = ot[:, tq:].T.astype(BF16)
    else:
        lp = lam_ref[...]
        lam = (jnp.exp(jnp.sum(lp[0:1] * lp[1:2], axis=1, keepdims=True))
               - jnp.exp(jnp.sum(lp[2:3] * lp[3:4], axis=1, keepdims=True)) + lam_init)
        d = (ot[:, :tq] - lam * ot[:, tq:]).T
        o_ref[...] = (_rms(d, subln_ref[...]) * (1.0 - lam_init)).astype(BF16)


def _flash(q_arr, k_arr, vt, *, mode, n_steps, q_blk, q_off, k_blk, k_off, out_w,
           tq=512, lam=None, subln=None, lam_init=0.0):
    s = q_arr.shape[0]
    tq = min(tq, s)
    _, nk, dv, tk = vt.shape
    unroll = math.gcd(nk, FLASH_UNROLL)
    assert unroll % 2 == 0
    ncols = tq if mode == "single" else 2 * tq
    in_specs = [pl.BlockSpec((tq, q_blk), lambda h, i: (i, h + q_off)),
                pl.BlockSpec((s, k_blk), lambda h, i: (0, h + k_off)),
                pl.BlockSpec((None, nk, dv, tk), lambda h, i: (h, 0, 0, 0))]
    args = [q_arr, k_arr, vt]
    if mode == "diff":
        in_specs += [pl.BlockSpec(lam.shape, lambda h, i: (0, 0)),
                     pl.BlockSpec((1, HEAD_DIM), lambda h, i: (0, 0))]
        args += [lam, subln.reshape(1, HEAD_DIM)]
    return pl.pallas_call(
        functools.partial(_flash_kernel, mode=mode, tq=tq, tk=tk, unroll=unroll,
                          lam_init=lam_init),
        grid=(n_steps, s // tq),
        in_specs=in_specs,
        out_specs=pl.BlockSpec((tq, out_w), lambda h, i: (i, h)),
        out_shape=jax.ShapeDtypeStruct((s, n_steps * out_w), BF16),
        scratch_shapes=[pltpu.VMEM((k_blk, ncols), BF16),
                        pltpu.VMEM((tk, ncols), F32), pltpu.VMEM((tk, ncols), F32),
                        pltpu.VMEM((tk, ncols), BF16),
                        pltpu.VMEM((dv, ncols), F32)],
        compiler_params=_cparams(2),
        name="flash_" + mode,
    )(*args)


def _swa_kernel(sink_ref, q_ref, k_ref, v_ref, o_ref, *, tq, win, scale):
    g = pl.program_id(0)
    i = pl.program_id(1)
    s = k_ref.shape[0]
    q0 = i * tq
    k0 = pl.multiple_of(jnp.clip(q0 - SWA_WINDOW, 0, s - win), LANES)
    kw = k_ref[pl.ds(k0, win), :]
    vw = v_ref[pl.ds(k0, win), :]
    qpos = q0 + lax.broadcasted_iota(jnp.int32, (tq, win), 0)
    kpos = k0 + lax.broadcasted_iota(jnp.int32, (tq, win), 1)
    valid = jnp.abs(kpos - qpos) <= SWA_WINDOW
    group = SWA_HEADS // SWA_KV_HEADS
    for j in range(group):
        q = q_ref[:, j * HEAD_DIM:(j + 1) * HEAD_DIM]
        sc = lax.dot_general(q, kw, (((1,), (1,)), ((), ())),
                             preferred_element_type=F32) * scale
        sc = jnp.where(valid, sc, NEG_BIG)
        sink = sink_ref[g * group + j]
        m = jnp.maximum(jnp.max(sc, axis=1, keepdims=True), sink)
        p = jnp.exp(sc - m)
        denom = jnp.sum(p, axis=1, keepdims=True) + jnp.exp(sink - m)
        pr = (p * (1.0 / denom)).astype(BF16)
        o_ref[:, j * HEAD_DIM:(j + 1) * HEAD_DIM] = jnp.dot(
            pr, vw, preferred_element_type=F32).astype(BF16)


def _swa(z, sink, tq=512):
    s = z.shape[0]
    tq = min(tq, s)
    win = min(tq + 2 * SWA_WINDOW, s)
    group = SWA_HEADS // SWA_KV_HEADS
    qw = group * HEAD_DIM
    k_off = SWA_HEADS
    v_off = SWA_HEADS + SWA_KV_HEADS
    return pl.pallas_call(
        functools.partial(_swa_kernel, tq=tq, win=win, scale=HEAD_DIM ** -0.5),
        grid=(SWA_KV_HEADS, s // tq),
        in_specs=[pl.BlockSpec(memory_space=pltpu.SMEM),
                  pl.BlockSpec((tq, qw), lambda g, i: (i, g)),
                  pl.BlockSpec((s, HEAD_DIM), lambda g, i: (0, g + k_off)),
                  pl.BlockSpec((s, HEAD_DIM), lambda g, i: (0, g + v_off))],
        out_specs=pl.BlockSpec((tq, qw), lambda g, i: (i, g)),
        out_shape=jax.ShapeDtypeStruct((s, SWA_HEADS * HEAD_DIM), BF16),
        compiler_params=_cparams(2),
        name="swa",
    )(sink, z, z, z)


def _merge_kernel(h_ref, b0_ref, b1_ref, b2_ref, b3_ref, wg_ref, wb_ref, o_ref):
    h = h_ref[...]
    acc = None
    for j, b_ref in enumerate((b0_ref, b1_ref, b2_ref, b3_ref)):
        gate = _sigmoid(jnp.dot(h, wg_ref[j], preferred_element_type=F32))
        term = gate * jnp.dot(b_ref[...], wb_ref[j], preferred_element_type=F32)
        acc = term if acc is None else acc + term
    o_ref[...] = acc.astype(BF16)


def _merge(h, branches, wg, wb, tm=512, tn=512):
    s, d = h.shape
    n = wg.shape[2]
    bw = wb.shape[1]
    tm = min(tm, s)
    b_spec = pl.BlockSpec((tm, bw), lambda j, i: (i, 0))
    return pl.pallas_call(
        _merge_kernel,
        grid=(n // tn, s // tm),
        in_specs=[pl.BlockSpec((tm, d), lambda j, i: (i, 0)),
                  b_spec, b_spec, b_spec, b_spec,
                  pl.BlockSpec((N_BRANCH, d, tn), lambda j, i: (0, 0, j)),
                  pl.BlockSpec((N_BRANCH, bw, tn), lambda j, i: (0, 0, j))],
        out_specs=pl.BlockSpec((tm, tn), lambda j, i: (i, j)),
        out_shape=jax.ShapeDtypeStruct((s, n), BF16),
        compiler_params=_cparams(2),
        name="merge",
    )(h, *branches, wg, wb)


def _attn_out_kernel(m_ref, w_ref, x_ref, gpost_ref, gnext_ref, xo_ref, ho_ref):
    y = jnp.dot(m_ref[...], w_ref[...], preferred_element_type=F32)
    xn = x_ref[...] + _rms(y, gpost_ref[...])
    xo_ref[...] = xn
    ho_ref[...] = _rms(xn, gnext_ref[...]).astype(BF16)


def _attn_out(merged, w, x, gpost, gnext, tm=512):
    s, d = x.shape
    tm = min(tm, s)
    row = pl.BlockSpec((tm, d), lambda i: (i, 0))
    vec = pl.BlockSpec((1, d), lambda i: (0, 0))
    return pl.pallas_call(
        _attn_out_kernel,
        grid=(s // tm,),
        in_specs=[row, _resident((d, d)), row, vec, vec],
        out_specs=[row, row],
        out_shape=[jax.ShapeDtypeStruct((s, d), F32), jax.ShapeDtypeStruct((s, d), BF16)],
        compiler_params=_cparams(1),
        name="attn_out",
    )(merged, w, x, gpost.reshape(1, d), gnext.reshape(1, d))


def _ffn_in_kernel(h_ref, wg_ref, wu_ref, o_ref):
    h = h_ref[...]
    gate = jnp.dot(h, wg_ref[...], preferred_element_type=F32)
    up = jnp.dot(h, wu_ref[...], preferred_element_type=F32)
    o_ref[...] = (gate * _sigmoid(gate) * up).astype(BF16)


def _ffn_in(h, w, tm=1024, tn=512):
    s, d = h.shape
    f = w.shape[1] // 2
    tm = min(tm, s)
    nt = f // tn
    return pl.pallas_call(
        _ffn_in_kernel,
        grid=(nt, s // tm),
        in_specs=[pl.BlockSpec((tm, d), lambda j, i: (i, 0)),
                  pl.BlockSpec((d, tn), lambda j, i: (0, j)),
                  pl.BlockSpec((d, tn), lambda j, i: (0, j + nt))],
        out_specs=pl.BlockSpec((tm, tn), lambda j, i: (i, j)),
        out_shape=jax.ShapeDtypeStruct((s, f), BF16),
        compiler_params=_cparams(2),
        name="ffn_in",
    )(h, w, w)


def _ffn_out_kernel(u_ref, w_ref, x_ref, g_ref, xo_ref, xb_ref):
    y = jnp.dot(u_ref[...], w_ref[...], preferred_element_type=F32)
    xn = x_ref[...] + _rms(y, g_ref[...])
    xo_ref[...] = xn
    xb_ref[...] = xn.astype(BF16)


def _ffn_out(u, w, x, g, tm=256):
    s, d = x.shape
    f = u.shape[1]
    tm = min(tm, s)
    row = pl.BlockSpec((tm, d), lambda i: (i, 0))
    return pl.pallas_call(
        _ffn_out_kernel,
        grid=(s // tm,),
        in_specs=[pl.BlockSpec((tm, f), lambda i: (i, 0)),
                  _resident((f, d)),
                  row, pl.BlockSpec((1, d), lambda i: (0, 0))],
        out_specs=[row, row],
        out_shape=[jax.ShapeDtypeStruct((s, d), F32), jax.ShapeDtypeStruct((s, d), BF16)],
        compiler_params=_cparams(1),
        name="ffn_out",
    )(u, w, x, g.reshape(1, d))


def _ple_kernel(xb_ref, wg_ref, p_ref, wp_ref, x_ref, gpost_ref, *rest):
    gate = _sigmoid(jnp.dot(xb_ref[...], wg_ref[...], preferred_element_type=F32))
    emb = jnp.dot(p_ref[...].astype(BF16), wp_ref[...], preferred_element_type=F32)
    xn = x_ref[...] + _rms(gate * emb, gpost_ref[...])
    if len(rest) == 3:
        gnext_ref, xo_ref, ho_ref = rest
        ho_ref[...] = _rms(xn, gnext_ref[...]).astype(BF16)
    else:
        (xo_ref,) = rest
    xo_ref[...] = xn


def _ple(xb, wg, p, wp, x, gpost, gnext, tm=512):
    s, d = x.shape
    pd = p.shape[1]
    tm = min(tm, s)
    row = pl.BlockSpec((tm, d), lambda i: (i, 0))
    vec = pl.BlockSpec((1, d), lambda i: (0, 0))
    in_specs = [row, _resident((d, d)), pl.BlockSpec((tm, pd), lambda i: (i, 0)),
                _resident((pd, d)), row, vec]
    args = [xb, wg, p, wp, x, gpost.reshape(1, d)]
    out_specs = [row]
    out_shape = [jax.ShapeDtypeStruct((s, d), F32)]
    if gnext is not None:
        in_specs.append(vec)
        args.append(gnext.reshape(1, d))
        out_specs.append(row)
        out_shape.append(jax.ShapeDtypeStruct((s, d), BF16))
    return pl.pallas_call(
        _ple_kernel,
        grid=(s // tm,),
        in_specs=in_specs,
        out_specs=out_specs,
        out_shape=out_shape,
        compiler_params=_cparams(1),
        name="ple",
    )(*args)


def _pad_cols(w, to):
    return jnp.pad(w, ((0, 0), (0, to - w.shape[1])))


def kernel(x, p, norm_mix_pre, norm_mix_post, norm_ffn_pre, norm_ffn_post, norm_ple_post, w_in, mla_qa_norm, mla_w_uq, mla_kva_norm, mla_w_ukv, diff_lambda, diff_subln, swa_sink, ax_q_norm, ax_k_norm, w_branch, w_branch_gate, w_o, w_ffn_in, w_ffn_out, w_ple, w_ple_gate):
    b, s, d = x.shape
    assert b == 1
    depth = w_in.shape[0]
    x2 = x.reshape(s, d)
    tab64, tab128, tab_ax = _position_tables(s)
    tk = min(FLASH_TK, s)

    c1 = MLA_Q_LORA + MLA_KV_LORA + MLA_ROPE

    t64, t128, tax = 0, 1, 2
    qs_b = DIFF_QK ** -0.5 * LOG2E
    qs_d = HEAD_DIM ** -0.5 * LOG2E
    plan = ([(t64, DIFF_QK // 2, None, qs_b, "zb", j) for j in range(4)]
            + [(t64, DIFF_QK // 2, None, 1.0, "zb", 4 + j) for j in range(4)]
            + [(None, 0, None, 1.0, "vtb", j) for j in range(4)]
            + [(t128, HEAD_DIM // 2, None, 1.0, "zc", j) for j in range(6)]
            + [(None, 0, None, 1.0, "zc", 6 + j) for j in range(2)]
            + [(tax, HEAD_DIM // 4, 0, qs_d, "zd", j) for j in range(4)]
            + [(tax, HEAD_DIM // 4, 1, 1.0, "zd", 4 + j) for j in range(2)]
            + [(None, 0, None, 1.0, "vtd", j) for j in range(2)])
    plan = tuple(plan)

    h = _prenorm(x2, norm_mix_pre[0])
    for i in range(depth):
        lam_init = 0.8 - 0.6 * math.exp(-0.3 * i)
        wi = w_in[i].astype(BF16)
        w_mla = _pad_cols(wi[:, :c1], MLA_Q_LORA + MLA_KV_LORA + LANES)

        cq, ckv, kr = _mla_in(h, w_mla, tab64, mla_qa_norm[i], mla_kva_norm[i])
        wq = jnp.pad(mla_w_uq[i].reshape(MLA_Q_LORA, MLA_HEADS, MLA_NOPE + MLA_ROPE),
                     ((0, 0), (0, 0), (0, MLA_QK_PAD - MLA_NOPE - MLA_ROPE))
                     ).reshape(MLA_Q_LORA, MLA_HEADS * MLA_QK_PAD).astype(BF16)
        wkv = mla_w_ukv[i].reshape(MLA_KV_LORA, MLA_HEADS, MLA_NOPE + MLA_V)
        wk = wkv[:, :, :MLA_NOPE].reshape(MLA_KV_LORA, MLA_HEADS * MLA_NOPE).astype(BF16)
        wv = wkv[:, :, MLA_NOPE:].reshape(MLA_KV_LORA, MLA_HEADS * MLA_V).astype(BF16)
        qa, ka, vta = _mla_up(cq, ckv, kr, wq, wk, wv, tab64, tk)
        br_a = _flash(qa, ka, vta, mode="single",
                      n_steps=MLA_HEADS, q_blk=MLA_QK_PAD, q_off=0, k_blk=MLA_QK_PAD, k_off=0,
                      out_w=MLA_V, tq=1024)

        gains_ax = jnp.stack([ax_q_norm[i], ax_k_norm[i]])
        zb, vtb, zc, zd, vtd = _inproj(h, wi[:, c1:], (tab64, tab128, tab_ax), gains_ax, plan, tk)

        br_b = _flash(zb, zb, vtb, mode="diff",
                      n_steps=DIFF_HEADS, q_blk=HEAD_DIM, q_off=0, k_blk=HEAD_DIM,
                      k_off=DIFF_HEADS, out_w=HEAD_DIM,
                      lam=diff_lambda[i], subln=diff_subln[i], lam_init=lam_init)

        br_c = _swa(zc, swa_sink[i])

        group = AX_HEADS // AX_KV_HEADS
        br_d = _flash(zd, zd, vtd, mode="pair",
                      n_steps=AX_KV_HEADS, q_blk=group * HEAD_DIM, q_off=0, k_blk=HEAD_DIM,
                      k_off=AX_HEADS, out_w=group * HEAD_DIM)

        merged = _merge(h, (br_a, br_b, br_c, br_d),
                        w_branch_gate[i].astype(BF16), w_branch[i].astype(BF16))
        x2, h2 = _attn_out(merged, w_o[i].astype(BF16), x2, norm_mix_post[i], norm_ffn_pre[i])

        u = _ffn_in(h2, w_ffn_in[i].astype(BF16))
        x2, xb = _ffn_out(u, w_ffn_out[i].astype(BF16), x2, norm_ffn_post[i])

        g_next = norm_mix_pre[i + 1] if i + 1 < depth else None
        outs = _ple(xb, w_ple_gate[i].astype(BF16), p[i, 0], w_ple[i].astype(BF16), x2,
                    norm_ple_post[i], g_next)
        x2 = outs[0]
        h = outs[1] if g_next is not None else None
    return x2.reshape(b, s, d)
```

```python
import functools
import math

import jax
import jax.numpy as jnp
from jax import lax
from jax.experimental import pallas as pl
from jax.experimental.pallas import tpu as pltpu

HEAD_DIM = 128
ROPE_THETA = 10000.0
RMS_EPS = 1e-6
GRID_W = 64
MLA_HEADS = 4
MLA_Q_LORA = 512
MLA_KV_LORA = 512
MLA_NOPE = 128
MLA_ROPE = 64
MLA_V = 128
DIFF_HEADS = 4
DIFF_QK = HEAD_DIM // 2
SWA_HEADS = 4
SWA_KV_HEADS = 2
SWA_WINDOW = 128
AX_HEADS = 4
AX_KV_HEADS = 2
N_BRANCH = 4

LANES = 128
SUBLANES = 8
BF16_ROWS = 16
MLA_QK_PAD = 2 * LANES
LOG2E = 1.4426950408889634
NEG_BIG = -1e30
MLA_QSCALE = (MLA_NOPE + MLA_ROPE) ** -0.5 * LOG2E
FLASH_TK = 1024
FLASH_PIECE = 256
FLASH_UNROLL = 8
VMEM_LIMIT = 56 * 1024 * 1024

BF16 = jnp.bfloat16
F32 = jnp.float32


def _cparams(n_grid):
    return pltpu.CompilerParams(
        dimension_semantics=("arbitrary",) * n_grid,
        vmem_limit_bytes=VMEM_LIMIT,
    )


def _resident(shape):
    return pl.BlockSpec(shape, lambda *_: (0,) * len(shape), pipeline_mode=pl.Buffered(1))


def _rms(y, g):
    return y * lax.rsqrt(jnp.mean(y * y, axis=-1, keepdims=True) + RMS_EPS) * g


def _rope(x, c, s1, s2, half):
    return (x * c + pltpu.roll(x, LANES - half, 1) * s1
            + pltpu.roll(x, half, 1) * s2)


def _sigmoid(x):
    return 1.0 / (1.0 + jnp.exp(-x))


def _position_tables(s):
    t = jnp.arange(s, dtype=F32)
    rows = s // GRID_W
    row_pos = jnp.broadcast_to(jnp.arange(rows)[:, None], (rows, GRID_W)).reshape(s).astype(F32)
    col_pos = jnp.broadcast_to(jnp.arange(GRID_W)[None, :], (rows, GRID_W)).reshape(s).astype(F32)
    inv64 = ROPE_THETA ** (-jnp.arange(0, 64, 2, dtype=F32) / 64)
    inv128 = ROPE_THETA ** (-jnp.arange(0, 128, 2, dtype=F32) / 128)
    lane = jnp.arange(LANES)
    first32 = (lane % 64) < 32
    first64 = lane < 64

    def tables(ang, first):
        c, sn = jnp.cos(ang), jnp.sin(ang)
        return (c, jnp.where(first[None, :], -sn, 0.0), jnp.where(first[None, :], 0.0, sn))

    ang64 = t[:, None] * inv64[lane % 32][None, :]
    ang128 = t[:, None] * inv128[lane % 64][None, :]
    pos_ax = jnp.where(first64[None, :], row_pos[:, None], col_pos[:, None])
    ang_ax = pos_ax * inv64[lane % 32][None, :]
    return tables(ang64, first32), tables(ang128, first64), tables(ang_ax, first32)


def _prenorm_kernel(x_ref, g_ref, o_ref):
    o_ref[...] = _rms(x_ref[...], g_ref[...]).astype(BF16)


def _prenorm(x, g, tm=512):
    s, d = x.shape
    tm = min(tm, s)
    return pl.pallas_call(
        _prenorm_kernel,
        grid=(s // tm,),
        in_specs=[pl.BlockSpec((tm, d), lambda i: (i, 0)),
                  pl.BlockSpec((1, d), lambda i: (0, 0))],
        out_specs=pl.BlockSpec((tm, d), lambda i: (i, 0)),
        out_shape=jax.ShapeDtypeStruct((s, d), BF16),
        compiler_params=_cparams(1),
        name="prenorm",
    )(x, g.reshape(1, d))


def _vt_spec(heads, dv, tm, tk):
    r = tk // tm
    return pl.BlockSpec((heads, None, dv, tm), lambda i: (0, i // r, 0, i % r))


def _inproj_kernel(h_ref, w_ref, *refs, plan):
    tab_refs, g_ref = refs[:9], refs[9]
    outs = dict(zip(("zb", "vtb", "zc", "zd", "vtd"), refs[10:]))
    acc = jnp.dot(h_ref[...], w_ref[...], preferred_element_type=F32)
    for b, (tab, half, gi, scale, dest, db) in enumerate(plan):
        blk = acc[:, b * LANES:(b + 1) * LANES]
        if gi is not None:
            blk = _rms(blk, g_ref[gi:gi + 1, :])
        if tab is not None:
            c_ref, s1_ref, s2_ref = tab_refs[3 * tab:3 * tab + 3]
            blk = _rope(blk, c_ref[...], s1_ref[...], s2_ref[...], half)
        if scale != 1.0:
            blk = blk * scale
        if dest in ("vtb", "vtd"):
            outs[dest][db] = blk.T.astype(BF16)
        else:
            outs[dest][:, db * LANES:(db + 1) * LANES] = blk.astype(BF16)


def _inproj(h, w, tabs, gains, plan, tk, tm=512):
    s, d = h.shape
    n = w.shape[1]
    tm = min(tm, s)
    nk = s // tk
    tab_spec = pl.BlockSpec((tm, LANES), lambda i: (i, 0))
    nb = 2 * DIFF_HEADS * HEAD_DIM
    nc = (SWA_HEADS + 2 * SWA_KV_HEADS) * HEAD_DIM
    nd = (AX_HEADS + AX_KV_HEADS) * HEAD_DIM
    row = lambda width: pl.BlockSpec((tm, width), lambda i: (i, 0))
    return pl.pallas_call(
        functools.partial(_inproj_kernel, plan=plan),
        grid=(s // tm,),
        in_specs=[pl.BlockSpec((tm, d), lambda i: (i, 0)), _resident((d, n))]
                 + [tab_spec] * 9 + [_resident(gains.shape)],
        out_specs=[row(nb), _vt_spec(DIFF_HEADS, HEAD_DIM, tm, tk), row(nc),
                   row(nd), _vt_spec(AX_KV_HEADS, HEAD_DIM, tm, tk)],
        out_shape=[jax.ShapeDtypeStruct((s, nb), BF16),
                   jax.ShapeDtypeStruct((DIFF_HEADS, nk, HEAD_DIM, tk), BF16),
                   jax.ShapeDtypeStruct((s, nc), BF16),
                   jax.ShapeDtypeStruct((s, nd), BF16),
                   jax.ShapeDtypeStruct((AX_KV_HEADS, nk, HEAD_DIM, tk), BF16)],
        compiler_params=_cparams(1),
        name="inproj",
    )(h, w, *[t for tab in tabs for t in tab], gains)


def _mla_in_kernel(h_ref, w_ref, c_ref, s1_ref, s2_ref, gq_ref, gkv_ref,
                   cq_ref, ckv_ref, kr_ref):
    acc = jnp.dot(h_ref[...], w_ref[...], preferred_element_type=F32)
    cq_ref[...] = _rms(acc[:, :MLA_Q_LORA], gq_ref[...]).astype(BF16)
    ckv_ref[...] = _rms(acc[:, MLA_Q_LORA:MLA_Q_LORA + MLA_KV_LORA], gkv_ref[...]).astype(BF16)
    kr = acc[:, MLA_Q_LORA + MLA_KV_LORA:]
    kr_ref[...] = _rope(kr, c_ref[...], s1_ref[...], s2_ref[...], MLA_ROPE // 2).astype(BF16)


def _mla_in(h, w, tabs, gq, gkv, tm=512):
    s, d = h.shape
    n = w.shape[1]
    tm = min(tm, s)
    c, s1, s2 = tabs
    tab_spec = pl.BlockSpec((tm, LANES), lambda i: (i, 0))
    return pl.pallas_call(
        _mla_in_kernel,
        grid=(s // tm,),
        in_specs=[pl.BlockSpec((tm, d), lambda i: (i, 0)),
                  pl.BlockSpec((d, n), lambda i: (0, 0)),
                  tab_spec, tab_spec, tab_spec,
                  pl.BlockSpec((1, MLA_Q_LORA), lambda i: (0, 0)),
                  pl.BlockSpec((1, MLA_KV_LORA), lambda i: (0, 0))],
        out_specs=[pl.BlockSpec((tm, MLA_Q_LORA), lambda i: (i, 0)),
                   pl.BlockSpec((tm, MLA_KV_LORA), lambda i: (i, 0)),
                   pl.BlockSpec((tm, LANES), lambda i: (i, 0))],
        out_shape=[jax.ShapeDtypeStruct((s, MLA_Q_LORA), BF16),
                   jax.ShapeDtypeStruct((s, MLA_KV_LORA), BF16),
                   jax.ShapeDtypeStruct((s, LANES), BF16)],
        compiler_params=_cparams(1),
        name="mla_in",
    )(h, w, c, s1, s2, gq.reshape(1, -1), gkv.reshape(1, -1))


def _mla_up_kernel(cq_ref, ckv_ref, kr_ref, wq_ref, wk_ref, wv_ref,
                   c_ref, s1_ref, s2_ref, q_ref, k_ref, vt_ref):
    c, s1, s2 = c_ref[...], s1_ref[...], s2_ref[...]
    q = jnp.dot(cq_ref[...], wq_ref[...], preferred_element_type=F32)
    kn = jnp.dot(ckv_ref[...], wk_ref[...], preferred_element_type=F32)
    kr = kr_ref[...]
    for hd in range(MLA_HEADS):
        lo = hd * MLA_QK_PAD
        q_ref[:, lo:lo + LANES] = (q[:, lo:lo + LANES] * MLA_QSCALE).astype(BF16)
        q_ref[:, lo + LANES:lo + 2 * LANES] = (_rope(
            q[:, lo + LANES:lo + 2 * LANES], c, s1, s2, MLA_ROPE // 2) * MLA_QSCALE).astype(BF16)
        k_ref[:, lo:lo + LANES] = kn[:, hd * LANES:(hd + 1) * LANES].astype(BF16)
        k_ref[:, lo + LANES:lo + 2 * LANES] = kr
    v = jnp.dot(ckv_ref[...], wv_ref[...], preferred_element_type=F32)
    for hd in range(MLA_HEADS):
        vt_ref[hd] = v[:, hd * MLA_V:(hd + 1) * MLA_V].T.astype(BF16)


def _mla_up(cq, ckv, kr, wq, wk, wv, tabs, tk, tm=512):
    s = cq.shape[0]
    tm = min(tm, s)
    c, s1, s2 = tabs
    nq = MLA_HEADS * MLA_QK_PAD
    tab_spec = pl.BlockSpec((tm, LANES), lambda i: (i, 0))
    return pl.pallas_call(
        _mla_up_kernel,
        grid=(s // tm,),
        in_specs=[pl.BlockSpec((tm, MLA_Q_LORA), lambda i: (i, 0)),
                  pl.BlockSpec((tm, MLA_KV_LORA), lambda i: (i, 0)),
                  pl.BlockSpec((tm, LANES), lambda i: (i, 0)),
                  pl.BlockSpec(wq.shape, lambda i: (0, 0)),
                  pl.BlockSpec(wk.shape, lambda i: (0, 0)),
                  pl.BlockSpec(wv.shape, lambda i: (0, 0)),
                  tab_spec, tab_spec, tab_spec],
        out_specs=[pl.BlockSpec((tm, nq), lambda i: (i, 0)),
                   pl.BlockSpec((tm, nq), lambda i: (i, 0)),
                   _vt_spec(MLA_HEADS, MLA_V, tm, tk)],
        out_shape=[jax.ShapeDtypeStruct((s, nq), BF16),
                   jax.ShapeDtypeStruct((s, nq), BF16),
                   jax.ShapeDtypeStruct((MLA_HEADS, s // tk, MLA_V, tk), BF16)],
        compiler_params=_cparams(1),
        name="mla_up",
    )(cq, ckv, kr, wq, wk, wv, c, s1, s2)


def _flash_kernel(*refs, mode, tq, tk, unroll, lam_init):
    if mode == "diff":
        (q_ref, k_ref, vt_ref, lam_ref, subln_ref, o_ref,
         qcat_ref, s0_ref, s1_ref, p_ref, acc_ref) = refs
    else:
        (q_ref, k_ref, vt_ref, o_ref,
         qcat_ref, s0_ref, s1_ref, p_ref, acc_ref) = refs
    qt = q_ref[...].astype(F32).T
    if mode == "single":
        qcat_ref[...] = qt.astype(BF16)
    elif mode == "pair":
        qcat_ref[:, :tq] = qt[:LANES].astype(BF16)
        qcat_ref[:, tq:] = qt[LANES:].astype(BF16)
    else:
        row = lax.broadcasted_iota(jnp.int32, qt.shape, 0)
        qcat_ref[:, :tq] = jnp.where(row < DIFF_QK, qt, 0.0).astype(BF16)
        qcat_ref[:, tq:] = jnp.where(row >= DIFF_QK, qt, 0.0).astype(BF16)
    ncols = qcat_ref.shape[1]
    dv = vt_ref.shape[1]
    nk = vt_ref.shape[0]
    s_bufs = (s0_ref, s1_ref)
    n_pieces = tk // FLASH_PIECE

    def scores(ci, r, dst_ref):
        k0 = pl.multiple_of(ci * tk + r * FLASH_PIECE, FLASH_PIECE)
        sc = jnp.dot(k_ref[pl.ds(k0, FLASH_PIECE), :], qcat_ref[...],
                     preferred_element_type=F32)
        dst_ref[r * FLASH_PIECE:(r + 1) * FLASH_PIECE, :] = sc
        return jnp.max(sc, axis=0, keepdims=True)

    def probs(s_ref, r, mb, psum):
        for rr in range(FLASH_PIECE // BF16_ROWS):
            lo = r * FLASH_PIECE + rr * BF16_ROWS
            p0 = jnp.exp2(s_ref[lo:lo + SUBLANES, :] - mb)
            p1 = jnp.exp2(s_ref[lo + SUBLANES:lo + BF16_ROWS, :] - mb)
            psum = psum + p0 + p1
            p_ref[lo:lo + BF16_ROWS, :] = jnp.concatenate([p0, p1], axis=0).astype(BF16)
        return psum

    def pv_piece(ci, r):
        lo = r * FLASH_PIECE
        return jnp.dot(vt_ref.at[ci][:, lo:lo + FLASH_PIECE], p_ref[lo:lo + FLASH_PIECE, :],
                       preferred_element_type=F32)

    def body(j, carry):
        m, l, cmax = carry
        for b in range(unroll):
            ci = unroll * j + b
            nxt = jnp.minimum(ci + 1, nk - 1)
            m_new = jnp.maximum(m, cmax)
            alpha = jnp.exp2(m - m_new)
            mb = jnp.broadcast_to(m_new, (SUBLANES, ncols))
            psum = jnp.zeros((SUBLANES, ncols), F32)
            pv = None
            for r in range(n_pieces):
                cm = scores(nxt, r, s_bufs[(b + 1) % 2])
                cmax_next = cm if r == 0 else jnp.maximum(cmax_next, cm)
                psum = probs(s_bufs[b % 2], r, mb, psum)
                d = pv_piece(ci, r)
                pv = d if pv is None else pv + d
            acc_ref[...] = alpha * acc_ref[...] + pv
            l = alpha * l + jnp.sum(psum, axis=0, keepdims=True)
            m, cmax = m_new, cmax_next
        return m, l, cmax

    cmax0 = scores(0, 0, s_bufs[0])
    for r in range(1, n_pieces):
        cmax0 = jnp.maximum(cmax0, scores(0, r, s_bufs[0]))
    acc_ref[...] = jnp.zeros_like(acc_ref)
    init = (jnp.full((1, ncols), NEG_BIG, F32), jnp.zeros((1, ncols), F32), cmax0)
    _, l, _ = lax.fori_loop(0, nk // unroll, body, init)
    ot = acc_ref[...] * (1.0 / l)
    if mode == "single":
        o_ref[...] = ot.T.astype(BF16)
    elif mode == "pair":
        o_ref[:, :dv] = ot[:, :tq].T.astype(BF16)
        o_ref[:, dv:] = ot[:, tq:].T.astype(BF16)
    else:
        lp = lam_ref[...]
        lam = (jnp.exp(jnp.sum(lp[0:1] * lp[1:2], axis=1, keepdims=True))
               - jnp.exp(jnp.sum(lp[2:3] * lp[3:4], axis=1, keepdims=True)) + lam_init)
        d = (ot[:, :tq] - lam * ot[:, tq:]).T
        o_ref[...] = (_rms(d, subln_ref[...]) * (1.0 - lam_init)).astype(BF16)


def _flash(q_arr, k_arr, vt, *, mode, n_steps, q_blk, q_off, k_blk, k_off, out_w,
           tq=512, lam=None, subln=None, lam_init=0.0):
    s = q_arr.shape[0]
    tq = min(tq, s)
    _, nk, dv, tk = vt.shape
    unroll = math.gcd(nk, FLASH_UNROLL)
    assert unroll % 2 == 0
    ncols = tq if mode == "single" else 2 * tq
    in_specs = [pl.BlockSpec((tq, q_blk), lambda h, i: (i, h + q_off)),
                pl.BlockSpec((s, k_blk), lambda h, i: (0, h + k_off)),
                pl.BlockSpec((None, nk, dv, tk), lambda h, i: (h, 0, 0, 0))]
    args = [q_arr, k_arr, vt]
    if mode == "diff":
        in_specs += [pl.BlockSpec(lam.shape, lambda h, i: (0, 0)),
                     pl.BlockSpec((1, HEAD_DIM), lambda h, i: (0, 0))]
        args += [lam, subln.reshape(1, HEAD_DIM)]
    return pl.pallas_call(
        functools.partial(_flash_kernel, mode=mode, tq=tq, tk=tk, unroll=unroll,
                          lam_init=lam_init),
        grid=(n_steps, s // tq),
        in_specs=in_specs,
        out_specs=pl.BlockSpec((tq, out_w), lambda h, i: (i, h)),
        out_shape=jax.ShapeDtypeStruct((s, n_steps * out_w), BF16),
        scratch_shapes=[pltpu.VMEM((k_blk, ncols), BF16),
                        pltpu.VMEM((tk, ncols), F32), pltpu.VMEM((tk, ncols), F32),
                        pltpu.VMEM((tk, ncols), BF16),
                        pltpu.VMEM((dv, ncols), F32)],
        compiler_params=_cparams(2),
        name="flash_" + mode,
    )(*args)


def _swa_kernel(sink_ref, q_ref, k_ref, v_ref, o_ref, *, tq, win, scale):
    g = pl.program_id(0)
    i = pl.program_id(1)
    s = k_ref.shape[0]
    q0 = i * tq
    k0 = pl.multiple_of(jnp.clip(q0 - SWA_WINDOW, 0, s - win), LANES)
    kw = k_ref[pl.ds(k0, win), :]
    vw = v_ref[pl.ds(k0, win), :]
    qpos = q0 + lax.broadcasted_iota(jnp.int32, (tq, win), 0)
    kpos = k0 + lax.broadcasted_iota(jnp.int32, (tq, win), 1)
    valid = jnp.abs(kpos - qpos) <= SWA_WINDOW
    group = SWA_HEADS // SWA_KV_HEADS
    for j in range(group):
        q = q_ref[:, j * HEAD_DIM:(j + 1) * HEAD_DIM]
        sc = lax.dot_general(q, kw, (((1,), (1,)), ((), ())),
                             preferred_element_type=F32) * scale
        sc = jnp.where(valid, sc, NEG_BIG)
        sink = sink_ref[g * group + j]
        m = jnp.maximum(jnp.max(sc, axis=1, keepdims=True), sink)
        p = jnp.exp(sc - m)
        denom = jnp.sum(p, axis=1, keepdims=True) + jnp.exp(sink - m)
        pr = (p * (1.0 / denom)).astype(BF16)
        o_ref[:, j * HEAD_DIM:(j + 1) * HEAD_DIM] = jnp.dot(
            pr, vw, preferred_element_type=F32).astype(BF16)


def _swa(z, sink, tq=256):
    s = z.shape[0]
    tq = min(tq, s)
    win = min(tq + 2 * SWA_WINDOW, s)
    group = SWA_HEADS // SWA_KV_HEADS
    qw = group * HEAD_DIM
    k_off = SWA_HEADS
    v_off = SWA_HEADS + SWA_KV_HEADS
    return pl.pallas_call(
        functools.partial(_swa_kernel, tq=tq, win=win, scale=HEAD_DIM ** -0.5),
        grid=(SWA_KV_HEADS, s // tq),
        in_specs=[pl.BlockSpec(memory_space=pltpu.SMEM),
                  pl.BlockSpec((tq, qw), lambda g, i: (i, g)),
                  pl.BlockSpec((s, HEAD_DIM), lambda g, i: (0, g + k_off)),
                  pl.BlockSpec((s, HEAD_DIM), lambda g, i: (0, g + v_off))],
        out_specs=pl.BlockSpec((tq, qw), lambda g, i: (i, g)),
        out_shape=jax.ShapeDtypeStruct((s, SWA_HEADS * HEAD_DIM), BF16),
        compiler_params=_cparams(2),
        name="swa",
    )(sink, z, z, z)


def _merge_kernel(h_ref, b0_ref, b1_ref, b2_ref, b3_ref, wg_ref, wb_ref, o_ref):
    h = h_ref[...]
    acc = None
    for j, b_ref in enumerate((b0_ref, b1_ref, b2_ref, b3_ref)):
        gate = _sigmoid(jnp.dot(h, wg_ref[j], preferred_element_type=F32))
        term = gate * jnp.dot(b_ref[...], wb_ref[j], preferred_element_type=F32)
        acc = term if acc is None else acc + term
    o_ref[...] = acc.astype(BF16)


def _merge(h, branches, wg, wb, tm=1024, tn=512):
    s, d = h.shape
    n = wg.shape[2]
    bw = wb.shape[1]
    tm = min(tm, s)
    b_spec = pl.BlockSpec((tm, bw), lambda j, i: (i, 0))
    return pl.pallas_call(
        _merge_kernel,
        grid=(n // tn, s // tm),
        in_specs=[pl.BlockSpec((tm, d), lambda j, i: (i, 0)),
                  b_spec, b_spec, b_spec, b_spec,
                  pl.BlockSpec((N_BRANCH, d, tn), lambda j, i: (0, 0, j)),
                  pl.BlockSpec((N_BRANCH, bw, tn), lambda j, i: (0, 0, j))],
        out_specs=pl.BlockSpec((tm, tn), lambda j, i: (i, j)),
        out_shape=jax.ShapeDtypeStruct((s, n), BF16),
        compiler_params=_cparams(2),
        name="merge",
    )(h, *branches, wg, wb)


def _attn_out_kernel(m_ref, w_ref, x_ref, gpost_ref, gnext_ref, xo_ref, ho_ref):
    y = jnp.dot(m_ref[...], w_ref[...], preferred_element_type=F32)
    xn = x_ref[...] + _rms(y, gpost_ref[...])
    xo_ref[...] = xn
    ho_ref[...] = _rms(xn, gnext_ref[...]).astype(BF16)


def _attn_out(merged, w, x, gpost, gnext, tm=512):
    s, d = x.shape
    tm = min(tm, s)
    row = pl.BlockSpec((tm, d), lambda i: (i, 0))
    vec = pl.BlockSpec((1, d), lambda i: (0, 0))
    return pl.pallas_call(
        _attn_out_kernel,
        grid=(s // tm,),
        in_specs=[row, _resident((d, d)), row, vec, vec],
        out_specs=[row, row],
        out_shape=[jax.ShapeDtypeStruct((s, d), F32), jax.ShapeDtypeStruct((s, d), BF16)],
        compiler_params=_cparams(1),
        name="attn_out",
    )(merged, w, x, gpost.reshape(1, d), gnext.reshape(1, d))


def _ffn_in_kernel(h_ref, wg_ref, wu_ref, o_ref):
    h = h_ref[...]
    gate = jnp.dot(h, wg_ref[...], preferred_element_type=F32)
    up = jnp.dot(h, wu_ref[...], preferred_element_type=F32)
    o_ref[...] = (gate * _sigmoid(gate) * up).astype(BF16)


def _ffn_in(h, w, tm=2048, tn=512):
    s, d = h.shape
    f = w.shape[1] // 2
    tm = min(tm, s)
    nt = f // tn
    return pl.pallas_call(
        _ffn_in_kernel,
        grid=(nt, s // tm),
        in_specs=[pl.BlockSpec((tm, d), lambda j, i: (i, 0)),
                  pl.BlockSpec((d, tn), lambda j, i: (0, j)),
                  pl.BlockSpec((d, tn), lambda j, i: (0, j + nt))],
        out_specs=pl.BlockSpec((tm, tn), lambda j, i: (i, j)),
        out_shape=jax.ShapeDtypeStruct((s, f), BF16),
        compiler_params=_cparams(2),
        name="ffn_in",
    )(h, w, w)


def _ffn_out_kernel(u_ref, w_ref, x_ref, g_ref, xo_ref, xb_ref):
    y = jnp.dot(u_ref[...], w_ref[...], preferred_element_type=F32)
    xn = x_ref[...] + _rms(y, g_ref[...])
    xo_ref[...] = xn
    xb_ref[...] = xn.astype(BF16)


def _ffn_out(u, w, x, g, tm=256):
    s, d = x.shape
    f = u.shape[1]
    tm = min(tm, s)
    row = pl.BlockSpec((tm, d), lambda i: (i, 0))
    return pl.pallas_call(
        _ffn_out_kernel,
        grid=(s // tm,),
        in_specs=[pl.BlockSpec((tm, f), lambda i: (i, 0)),
                  _resident((f, d)),
                  row, pl.BlockSpec((1, d), lambda i: (0, 0))],
        out_specs=[row, row],
        out_shape=[jax.ShapeDtypeStruct((s, d), F32), jax.ShapeDtypeStruct((s, d), BF16)],
        compiler_params=_cparams(1),
        name="ffn_out",
    )(u, w, x, g.reshape(1, d))


def _ple_kernel(xb_ref, wg_ref, p_ref, wp_ref, x_ref, gpost_ref, *rest):
    gate = _sigmoid(jnp.dot(xb_ref[...], wg_ref[...], preferred_element_type=F32))
    emb = jnp.dot(p_ref[...].astype(BF16), wp_ref[...], preferred_element_type=F32)
    xn = x_ref[...] + _rms(gate * emb, gpost_ref[...])
    if len(rest) == 3:
        gnext_ref, xo_ref, ho_ref = rest
        ho_ref[...] = _rms(xn, gnext_ref[...]).astype(BF16)
    else:
        (xo_ref,) = rest
    xo_ref[...] = xn


def _ple(xb, wg, p, wp, x, gpost, gnext, tm=512):
    s, d = x.shape
    pd = p.shape[1]
    tm = min(tm, s)
    row = pl.BlockSpec((tm, d), lambda i: (i, 0))
    vec = pl.BlockSpec((1, d), lambda i: (0, 0))
    in_specs = [row, _resident((d, d)), pl.BlockSpec((tm, pd), lambda i: (i, 0)),
                _resident((pd, d)), row, vec]
    args = [xb, wg, p, wp, x, gpost.reshape(1, d)]
    out_specs = [row]
    out_shape = [jax.ShapeDtypeStruct((s, d), F32)]
    if gnext is not None:
        in_specs.append(vec)
        args.append(gnext.reshape(1, d))
        out_specs.append(row)
        out_shape.append(jax.ShapeDtypeStruct((s, d), BF16))
    return pl.pallas_call(
        _ple_kernel,
        grid=(s // tm,),
        in_specs=in_specs,
        out_specs=out_specs,
        out_shape=out_shape,
        compiler_params=_cparams(1),
        name="ple",
    )(*args)


def _pad_cols(w, to):
    return jnp.pad(w, ((0, 0), (0, to - w.shape[1])))


def kernel(x, p, norm_mix_pre, norm_mix_post, norm_ffn_pre, norm_ffn_post, norm_ple_post, w_in, mla_qa_norm, mla_w_uq, mla_kva_norm, mla_w_ukv, diff_lambda, diff_subln, swa_sink, ax_q_norm, ax_k_norm, w_branch, w_branch_gate, w_o, w_ffn_in, w_ffn_out, w_ple, w_ple_gate):
    b, s, d = x.shape
    assert b == 1
    depth = w_in.shape[0]
    x2 = x.reshape(s, d)
    tab64, tab128, tab_ax = _position_tables(s)
    tk = min(FLASH_TK, s)

    c1 = MLA_Q_LORA + MLA_KV_LORA + MLA_ROPE

    t64, t128, tax = 0, 1, 2
    qs_b = DIFF_QK ** -0.5 * LOG2E
    qs_d = HEAD_DIM ** -0.5 * LOG2E
    plan = ([(t64, DIFF_QK // 2, None, qs_b, "zb", j) for j in range(4)]
            + [(t64, DIFF_QK // 2, None, 1.0, "zb", 4 + j) for j in range(4)]
            + [(None, 0, None, 1.0, "vtb", j) for j in range(4)]
            + [(t128, HEAD_DIM // 2, None, 1.0, "zc", j) for j in range(6)]
            + [(None, 0, None, 1.0, "zc", 6 + j) for j in range(2)]
            + [(tax, HEAD_DIM // 4, 0, qs_d, "zd", j) for j in range(4)]
            + [(tax, HEAD_DIM // 4, 1, 1.0, "zd", 4 + j) for j in range(2)]
            + [(None, 0, None, 1.0, "vtd", j) for j in range(2)])
    plan = tuple(plan)

    h = _prenorm(x2, norm_mix_pre[0])
    for i in range(depth):
        lam_init = 0.8 - 0.6 * math.exp(-0.3 * i)
        wi = w_in[i].astype(BF16)
        w_mla = _pad_cols(wi[:, :c1], MLA_Q_LORA + MLA_KV_LORA + LANES)

        cq, ckv, kr = _mla_in(h, w_mla, tab64, mla_qa_norm[i], mla_kva_norm[i])
        wq = jnp.pad(mla_w_uq[i].reshape(MLA_Q_LORA, MLA_HEADS, MLA_NOPE + MLA_ROPE),
                     ((0, 0), (0, 0), (0, MLA_QK_PAD - MLA_NOPE - MLA_ROPE))
                     ).reshape(MLA_Q_LORA, MLA_HEADS * MLA_QK_PAD).astype(BF16)
        wkv = mla_w_ukv[i].reshape(MLA_KV_LORA, MLA_HEADS, MLA_NOPE + MLA_V)
        wk = wkv[:, :, :MLA_NOPE].reshape(MLA_KV_LORA, MLA_HEADS * MLA_NOPE).astype(BF16)
        wv = wkv[:, :, MLA_NOPE:].reshape(MLA_KV_LORA, MLA_HEADS * MLA_V).astype(BF16)
        qa, ka, vta = _mla_up(cq, ckv, kr, wq, wk, wv, tab64, tk)
        br_a = _flash(qa, ka, vta, mode="single",
                      n_steps=MLA_HEADS, q_blk=MLA_QK_PAD, q_off=0, k_blk=MLA_QK_PAD, k_off=0,
                      out_w=MLA_V, tq=1024)

        gains_ax = jnp.stack([ax_q_norm[i], ax_k_norm[i]])
        zb, vtb, zc, zd, vtd = _inproj(h, wi[:, c1:], (tab64, tab128, tab_ax), gains_ax, plan, tk)

        br_b = _flash(zb, zb, vtb, mode="diff",
                      n_steps=DIFF_HEADS, q_blk=HEAD_DIM, q_off=0, k_blk=HEAD_DIM,
                      k_off=DIFF_HEADS, out_w=HEAD_DIM,
                      lam=diff_lambda[i], subln=diff_subln[i], lam_init=lam_init)

        br_c = _swa(zc, swa_sink[i])

        group = AX_HEADS // AX_KV_HEADS
        br_d = _flash(zd, zd, vtd, mode="pair",
                      n_steps=AX_KV_HEADS, q_blk=group * HEAD_DIM, q_off=0, k_blk=HEAD_DIM,
                      k_off=AX_HEADS, out_w=group * HEAD_DIM)

        merged = _merge(h, (br_a, br_b, br_c, br_d),
                        w_branch_gate[i].astype(BF16), w_branch[i].astype(BF16))
        x2, h2 = _attn_out(merged, w_o[i].astype(BF16), x2, norm_mix_post[i], norm_ffn_pre[i])

        u = _ffn_in(h2, w_ffn_in[i].astype(BF16))
        x2, xb = _ffn_out(u, w_ffn_out[i].astype(BF16), x2, norm_ffn_post[i])

        g_next = norm_mix_pre[i + 1] if i + 1 < depth else None
        outs = _ple(xb, w_ple_gate[i].astype(BF16), p[i, 0], w_ple[i].astype(BF16), x2,
                    norm_ple_post[i], g_next)
        x2 = outs[0]
        h = outs[1] if g_next is not None else None
    return x2.reshape(b, s, d)
```

```python
import functools
import math

import jax
import jax.numpy as jnp
from jax import lax
from jax.experimental import pallas as pl
from jax.experimental.pallas import tpu as pltpu

HEAD_DIM = 128
ROPE_THETA = 10000.0
RMS_EPS = 1e-6
GRID_W = 64
MLA_HEADS = 4
MLA_Q_LORA = 512
MLA_KV_LORA = 512
MLA_NOPE = 128
MLA_ROPE = 64
MLA_V = 128
DIFF_HEADS = 4
DIFF_QK = HEAD_DIM // 2
SWA_HEADS = 4
SWA_KV_HEADS = 2
SWA_WINDOW = 128
AX_HEADS = 4
AX_KV_HEADS = 2
N_BRANCH = 4

LANES = 128
SUBLANES = 8
BF16_ROWS = 16
MLA_QK_PAD = 2 * LANES
LOG2E = 1.4426950408889634
MLA_QSCALE = (MLA_NOPE + MLA_ROPE) ** -0.5 * LOG2E
FLASH_TK = 1024
FLASH_PIECE = 256
FLASH_UNROLL = 8
VMEM_LIMIT = 56 * 1024 * 1024

BF16 = jnp.bfloat16
F32 = jnp.float32


def _cparams(n_grid):
    return pltpu.CompilerParams(
        dimension_semantics=("arbitrary",) * n_grid,
        vmem_limit_bytes=VMEM_LIMIT,
    )


def _resident(shape, layer=None):
    if layer is None:
        return pl.BlockSpec(shape, lambda *_: (0,) * len(shape), pipeline_mode=pl.Buffered(1))
    return pl.BlockSpec((None,) + tuple(shape), lambda *_: (layer,) + (0,) * len(shape),
                        pipeline_mode=pl.Buffered(1))


def _rms(y, g):
    return y * lax.rsqrt(jnp.mean(y * y, axis=-1, keepdims=True) + RMS_EPS) * g


def _rope(x, c, s1, s2, half):
    return (x * c + pltpu.roll(x, LANES - half, 1) * s1
            + pltpu.roll(x, half, 1) * s2)


def _sigmoid(x):
    return 1.0 / (1.0 + jnp.exp(-x))


def _position_tables(s):
    t = jnp.arange(s, dtype=F32)
    rows = s // GRID_W
    row_pos = jnp.broadcast_to(jnp.arange(rows)[:, None], (rows, GRID_W)).reshape(s).astype(F32)
    col_pos = jnp.broadcast_to(jnp.arange(GRID_W)[None, :], (rows, GRID_W)).reshape(s).astype(F32)
    inv64 = ROPE_THETA ** (-jnp.arange(0, 64, 2, dtype=F32) / 64)
    inv128 = ROPE_THETA ** (-jnp.arange(0, 128, 2, dtype=F32) / 128)
    lane = jnp.arange(LANES)
    first32 = (lane % 64) < 32
    first64 = lane < 64

    def tables(ang, first):
        c, sn = jnp.cos(ang), jnp.sin(ang)
        return (c, jnp.where(first[None, :], -sn, 0.0), jnp.where(first[None, :], 0.0, sn))

    ang64 = t[:, None] * inv64[lane % 32][None, :]
    ang128 = t[:, None] * inv128[lane % 64][None, :]
    pos_ax = jnp.where(first64[None, :], row_pos[:, None], col_pos[:, None])
    ang_ax = pos_ax * inv64[lane % 32][None, :]
    return tables(ang64, first32), tables(ang128, first64), tables(ang_ax, first32)


def _prenorm_kernel(x_ref, g_ref, o_ref):
    o_ref[...] = _rms(x_ref[...], g_ref[...]).astype(BF16)


def _prenorm(x, g, tm=512):
    s, d = x.shape
    tm = min(tm, s)
    return pl.pallas_call(
        _prenorm_kernel,
        grid=(s // tm,),
        in_specs=[pl.BlockSpec((tm, d), lambda i: (i, 0)),
                  pl.BlockSpec((1, d), lambda i: (0, 0))],
        out_specs=pl.BlockSpec((tm, d), lambda i: (i, 0)),
        out_shape=jax.ShapeDtypeStruct((s, d), BF16),
        compiler_params=_cparams(1),
        name="prenorm",
    )(x, g.reshape(1, d))


def _vt_spec(heads, dv, tm, tk):
    r = tk // tm
    return pl.BlockSpec((heads, None, dv, tm), lambda i: (0, i // r, 0, i % r))


def _inproj_kernel(h_ref, w_ref, *refs, plan):
    tab_refs, g_ref = refs[:9], refs[9]
    outs = dict(zip(("zb", "vtb", "zc", "zd", "vtd"), refs[10:]))
    acc = jnp.dot(h_ref[...], w_ref[...], preferred_element_type=F32)
    for b, (tab, half, gi, scale, dest, db) in enumerate(plan):
        blk = acc[:, b * LANES:(b + 1) * LANES]
        if gi is not None:
            blk = _rms(blk, g_ref[gi:gi + 1, :])
        if tab is not None:
            c_ref, s1_ref, s2_ref = tab_refs[3 * tab:3 * tab + 3]
            blk = _rope(blk, c_ref[...], s1_ref[...], s2_ref[...], half)
        if scale != 1.0:
            blk = blk * scale
        if dest in ("vtb", "vtd"):
            outs[dest][db] = blk.T.astype(BF16)
        else:
            outs[dest][:, db * LANES:(db + 1) * LANES] = blk.astype(BF16)


def _inproj(h, w, tabs, gains, plan, tk, tm=512):
    s, d = h.shape
    n = w.shape[1]
    tm = min(tm, s)
    nk = s // tk
    tab_spec = pl.BlockSpec((tm, LANES), lambda i: (i, 0))
    nb = 2 * DIFF_HEADS * HEAD_DIM
    nc = (SWA_HEADS + 2 * SWA_KV_HEADS) * HEAD_DIM
    nd = (AX_HEADS + AX_KV_HEADS) * HEAD_DIM
    row = lambda width: pl.BlockSpec((tm, width), lambda i: (i, 0))
    return pl.pallas_call(
        functools.partial(_inproj_kernel, plan=plan),
        grid=(s // tm,),
        in_specs=[pl.BlockSpec((tm, d), lambda i: (i, 0)), _resident((d, n))]
                 + [tab_spec] * 9 + [_resident(gains.shape)],
        out_specs=[row(nb), _vt_spec(DIFF_HEADS, HEAD_DIM, tm, tk), row(nc),
                   row(nd), _vt_spec(AX_KV_HEADS, HEAD_DIM, tm, tk)],
        out_shape=[jax.ShapeDtypeStruct((s, nb), BF16),
                   jax.ShapeDtypeStruct((DIFF_HEADS, nk, HEAD_DIM, tk), BF16),
                   jax.ShapeDtypeStruct((s, nc), BF16),
                   jax.ShapeDtypeStruct((s, nd), BF16),
                   jax.ShapeDtypeStruct((AX_KV_HEADS, nk, HEAD_DIM, tk), BF16)],
        compiler_params=_cparams(1),
        name="inproj",
    )(h, w, *[t for tab in tabs for t in tab], gains)


def _mla_in_kernel(h_ref, w_ref, c_ref, s1_ref, s2_ref, gq_ref, gkv_ref,
                   cq_ref, ckv_ref, kr_ref):
    acc = jnp.dot(h_ref[...], w_ref[...], preferred_element_type=F32)
    cq_ref[...] = _rms(acc[:, :MLA_Q_LORA], gq_ref[...]).astype(BF16)
    ckv_ref[...] = _rms(acc[:, MLA_Q_LORA:MLA_Q_LORA + MLA_KV_LORA], gkv_ref[...]).astype(BF16)
    kr = acc[:, MLA_Q_LORA + MLA_KV_LORA:]
    kr_ref[...] = _rope(kr, c_ref[...], s1_ref[...], s2_ref[...], MLA_ROPE // 2).astype(BF16)


def _mla_in(h, w, tabs, gq, gkv, tm=512):
    s, d = h.shape
    n = w.shape[1]
    tm = min(tm, s)
    c, s1, s2 = tabs
    tab_spec = pl.BlockSpec((tm, LANES), lambda i: (i, 0))
    return pl.pallas_call(
        _mla_in_kernel,
        grid=(s // tm,),
        in_specs=[pl.BlockSpec((tm, d), lambda i: (i, 0)),
                  pl.BlockSpec((d, n), lambda i: (0, 0)),
                  tab_spec, tab_spec, tab_spec,
                  pl.BlockSpec((1, MLA_Q_LORA), lambda i: (0, 0)),
                  pl.BlockSpec((1, MLA_KV_LORA), lambda i: (0, 0))],
        out_specs=[pl.BlockSpec((tm, MLA_Q_LORA), lambda i: (i, 0)),
                   pl.BlockSpec((tm, MLA_KV_LORA), lambda i: (i, 0)),
                   pl.BlockSpec((tm, LANES), lambda i: (i, 0))],
        out_shape=[jax.ShapeDtypeStruct((s, MLA_Q_LORA), BF16),
                   jax.ShapeDtypeStruct((s, MLA_KV_LORA), BF16),
                   jax.ShapeDtypeStruct((s, LANES), BF16)],
        compiler_params=_cparams(1),
        name="mla_in",
    )(h, w, c, s1, s2, gq.reshape(1, -1), gkv.reshape(1, -1))


def _mla_up_kernel(cq_ref, ckv_ref, kr_ref, wq_ref, wk_ref, wv_ref,
                   c_ref, s1_ref, s2_ref, q_ref, k_ref, vt_ref):
    c, s1, s2 = c_ref[...], s1_ref[...], s2_ref[...]
    q = jnp.dot(cq_ref[...], wq_ref[...], preferred_element_type=F32)
    kn = jnp.dot(ckv_ref[...], wk_ref[...], preferred_element_type=F32)
    kr = kr_ref[...]
    for hd in range(MLA_HEADS):
        lo = hd * MLA_QK_PAD
        q_ref[:, lo:lo + LANES] = (q[:, lo:lo + LANES] * MLA_QSCALE).astype(BF16)
        q_ref[:, lo + LANES:lo + 2 * LANES] = (_rope(
            q[:, lo + LANES:lo + 2 * LANES], c, s1, s2, MLA_ROPE // 2) * MLA_QSCALE).astype(BF16)
        k_ref[:, lo:lo + LANES] = kn[:, hd * LANES:(hd + 1) * LANES].astype(BF16)
        k_ref[:, lo + LANES:lo + 2 * LANES] = kr
    v = jnp.dot(ckv_ref[...], wv_ref[...], preferred_element_type=F32)
    for hd in range(MLA_HEADS):
        vt_ref[hd] = v[:, hd * MLA_V:(hd + 1) * MLA_V].T.astype(BF16)


def _mla_up(cq, ckv, kr, wq, wk, wv, tabs, tk, tm=512):
    s = cq.shape[0]
    tm = min(tm, s)
    c, s1, s2 = tabs
    nq = MLA_HEADS * MLA_QK_PAD
    tab_spec = pl.BlockSpec((tm, LANES), lambda i: (i, 0))
    return pl.pallas_call(
        _mla_up_kernel,
        grid=(s // tm,),
        in_specs=[pl.BlockSpec((tm, MLA_Q_LORA), lambda i: (i, 0)),
                  pl.BlockSpec((tm, MLA_KV_LORA), lambda i: (i, 0)),
                  pl.BlockSpec((tm, LANES), lambda i: (i, 0)),
                  pl.BlockSpec(wq.shape, lambda i: (0, 0)),
                  pl.BlockSpec(wk.shape, lambda i: (0, 0)),
                  pl.BlockSpec(wv.shape, lambda i: (0, 0)),
                  tab_spec, tab_spec, tab_spec],
        out_specs=[pl.BlockSpec((tm, nq), lambda i: (i, 0)),
                   pl.BlockSpec((tm, nq), lambda i: (i, 0)),
                   _vt_spec(MLA_HEADS, MLA_V, tm, tk)],
        out_shape=[jax.ShapeDtypeStruct((s, nq), BF16),
                   jax.ShapeDtypeStruct((s, nq), BF16),
                   jax.ShapeDtypeStruct((MLA_HEADS, s // tk, MLA_V, tk), BF16)],
        compiler_params=_cparams(1),
        name="mla_up",
    )(cq, ckv, kr, wq, wk, wv, c, s1, s2)


def _flash_kernel(*refs, mode, tq, tk, unroll, lam_init):
    if mode == "diff":
        (q_ref, k_ref, vt_ref, lam_ref, subln_ref, o_ref,
         qcat_ref, s0_ref, s1_ref, p_ref, acc_ref) = refs
    else:
        (q_ref, k_ref, vt_ref, o_ref,
         qcat_ref, s0_ref, s1_ref, p_ref, acc_ref) = refs
    qt = q_ref[...].astype(F32).T
    if mode == "single":
        qcat_ref[...] = qt.astype(BF16)
    elif mode == "pair":
        qcat_ref[:, :tq] = qt[:LANES].astype(BF16)
        qcat_ref[:, tq:] = qt[LANES:].astype(BF16)
    else:
        row = lax.broadcasted_iota(jnp.int32, qt.shape, 0)
        qcat_ref[:, :tq] = jnp.where(row < DIFF_QK, qt, 0.0).astype(BF16)
        qcat_ref[:, tq:] = jnp.where(row >= DIFF_QK, qt, 0.0).astype(BF16)
    ncols = qcat_ref.shape[1]
    dv = vt_ref.shape[1]
    nk = vt_ref.shape[0]
    s_bufs = (s0_ref, s1_ref)
    n_pieces = tk // FLASH_PIECE

    def scores(ci, r, dst_ref):
        k0 = pl.multiple_of(ci * tk + r * FLASH_PIECE, FLASH_PIECE)
        sc = jnp.dot(k_ref[pl.ds(k0, FLASH_PIECE), :], qcat_ref[...],
                     preferred_element_type=F32)
        dst_ref[r * FLASH_PIECE:(r + 1) * FLASH_PIECE, :] = sc
        return jnp.max(sc, axis=0, keepdims=True)

    def probs(s_ref, r, mb, psum):
        for rr in range(FLASH_PIECE // BF16_ROWS):
            lo = r * FLASH_PIECE + rr * BF16_ROWS
            p0 = jnp.exp2(s_ref[lo:lo + SUBLANES, :] - mb)
            p1 = jnp.exp2(s_ref[lo + SUBLANES:lo + BF16_ROWS, :] - mb)
            psum = psum + p0 + p1
            p_ref[lo:lo + BF16_ROWS, :] = jnp.concatenate([p0, p1], axis=0).astype(BF16)
        return psum

    def pv_piece(ci, r):
        lo = r * FLASH_PIECE
        return jnp.dot(vt_ref.at[ci][:, lo:lo + FLASH_PIECE], p_ref[lo:lo + FLASH_PIECE, :],
                       preferred_element_type=F32)

    def body(j, carry):
        m, l, cmax = carry
        for b in range(unroll):
            ci = unroll * j + b
            nxt = jnp.minimum(ci + 1, nk - 1)
            m_new = jnp.maximum(m, cmax)
            alpha = jnp.exp2(m - m_new)
            mb = jnp.broadcast_to(m_new, (SUBLANES, ncols))
            psum = jnp.zeros((SUBLANES, ncols), F32)
            pv = None
            for r in range(n_pieces):
                cm = scores(nxt, r, s_bufs[(b + 1) % 2])
                cmax_next = cm if r == 0 else jnp.maximum(cmax_next, cm)
                psum = probs(s_bufs[b % 2], r, mb, psum)
                d = pv_piece(ci, r)
                pv = d if pv is None else pv + d
            acc_ref[...] = alpha * acc_ref[...] + pv
            l = alpha * l + jnp.sum(psum, axis=0, keepdims=True)
            m, cmax = m_new, cmax_next
        return m, l, cmax

    cmax0 = scores(0, 0, s_bufs[0])
    for r in range(1, n_pieces):
        cmax0 = jnp.maximum(cmax0, scores(0, r, s_bufs[0]))
    acc_ref[...] = jnp.zeros_like(acc_ref)
    init = (jnp.full((1, ncols), -jnp.inf, F32), jnp.zeros((1, ncols), F32), cmax0)
    _, l, _ = lax.fori_loop(0, nk // unroll, body, init)
    ot = acc_ref[...] * (1.0 / l)
    if mode == "single":
        o_ref[...] = ot.T.astype(BF16)
    elif mode == "pair":
        o_ref[:, :dv] = ot[:, :tq].T.astype(BF16)
        o_ref[:, dv:] = ot[:, tq:].T.astype(BF16)
    else:
        lp = lam_ref[...]
        lam = (jnp.exp(jnp.sum(lp[0:1] * lp[1:2], axis=1, keepdims=True))
               - jnp.exp(jnp.sum(lp[2:3] * lp[3:4], axis=1, keepdims=True)) + lam_init)
        d = (ot[:, :tq] - lam * ot[:, tq:]).T
        o_ref[...] = (_rms(d, subln_ref[...]) * (1.0 - lam_init)).astype(BF16)


def _flash(q_arr, k_arr, vt, *, mode, n_steps, q_blk, q_off, k_blk, k_off, out_w,
           tq=512, lam=None, subln=None, lam_init=0.0):
    s = q_arr.shape[0]
    tq = min(tq, s)
    _, nk, dv, tk = vt.shape
    unroll = math.gcd(nk, FLASH_UNROLL)
    assert unroll % 2 == 0
    ncols = tq if mode == "single" else 2 * tq
    in_specs = [pl.BlockSpec((tq, q_blk), lambda h, i: (i, h + q_off)),
                pl.BlockSpec((s, k_blk), lambda h, i: (0, h + k_off)),
                pl.BlockSpec((None, nk, dv, tk), lambda h, i: (h, 0, 0, 0))]
    args = [q_arr, k_arr, vt]
    if mode == "diff":
        in_specs += [pl.BlockSpec(lam.shape, lambda h, i: (0, 0)),
                     pl.BlockSpec((1, HEAD_DIM), lambda h, i: (0, 0))]
        args += [lam, subln.reshape(1, HEAD_DIM)]
    return pl.pallas_call(
        functools.partial(_flash_kernel, mode=mode, tq=tq, tk=tk, unroll=unroll,
                          lam_init=lam_init),
        grid=(n_steps, s // tq),
        in_specs=in_specs,
        out_specs=pl.BlockSpec((tq, out_w), lambda h, i: (i, h)),
        out_shape=jax.ShapeDtypeStruct((s, n_steps * out_w), BF16),
        scratch_shapes=[pltpu.VMEM((k_blk, ncols), BF16),
                        pltpu.VMEM((tk, ncols), F32), pltpu.VMEM((tk, ncols), F32),
                        pltpu.VMEM((tk, ncols), BF16),
                        pltpu.VMEM((dv, ncols), F32)],
        compiler_params=_cparams(2),
        name="flash_" + mode,
    )(*args)


def _swa_kernel(sink_ref, q_ref, k_ref, v_ref, o_ref, *, tq, win, scale):
    g = pl.program_id(0)
    i = pl.program_id(1)
    s = k_ref.shape[0]
    q0 = i * tq
    k0 = pl.multiple_of(jnp.clip(q0 - SWA_WINDOW, 0, s - win), LANES)
    kw = k_ref[pl.ds(k0, win), :]
    vw = v_ref[pl.ds(k0, win), :]
    qpos = q0 + lax.broadcasted_iota(jnp.int32, (tq, win), 0)
    kpos = k0 + lax.broadcasted_iota(jnp.int32, (tq, win), 1)
    valid = jnp.abs(kpos - qpos) <= SWA_WINDOW
    group = SWA_HEADS // SWA_KV_HEADS
    for j in range(group):
        q = q_ref[:, j * HEAD_DIM:(j + 1) * HEAD_DIM]
        sc = lax.dot_general(q, kw, (((1,), (1,)), ((), ())),
                             preferred_element_type=F32) * scale
        sc = jnp.where(valid, sc, -jnp.inf)
        sink = sink_ref[g * group + j]
        m = jnp.maximum(jnp.max(sc, axis=1, keepdims=True), sink)
        p = jnp.exp(sc - m)
        denom = jnp.sum(p, axis=1, keepdims=True) + jnp.exp(sink - m)
        pr = (p * (1.0 / denom)).astype(BF16)
        o_ref[:, j * HEAD_DIM:(j + 1) * HEAD_DIM] = jnp.dot(
            pr, vw, preferred_element_type=F32).astype(BF16)


def _swa(z, sink, tq=512):
    s = z.shape[0]
    tq = min(tq, s)
    win = min(tq + 2 * SWA_WINDOW, s)
    group = SWA_HEADS // SWA_KV_HEADS
    qw = group * HEAD_DIM
    k_off = SWA_HEADS
    v_off = SWA_HEADS + SWA_KV_HEADS
    return pl.pallas_call(
        functools.partial(_swa_kernel, tq=tq, win=win, scale=HEAD_DIM ** -0.5),
        grid=(SWA_KV_HEADS, s // tq),
        in_specs=[pl.BlockSpec(memory_space=pltpu.SMEM),
                  pl.BlockSpec((tq, qw), lambda g, i: (i, g)),
                  pl.BlockSpec((s, HEAD_DIM), lambda g, i: (0, g + k_off)),
                  pl.BlockSpec((s, HEAD_DIM), lambda g, i: (0, g + v_off))],
        out_specs=pl.BlockSpec((tq, qw), lambda g, i: (i, g)),
        out_shape=jax.ShapeDtypeStruct((s, SWA_HEADS * HEAD_DIM), BF16),
        compiler_params=_cparams(2),
        name="swa",
    )(sink, z, z, z)


def _merge_kernel(h_ref, b0_ref, b1_ref, b2_ref, b3_ref, wg_ref, wb_ref, o_ref):
    h = h_ref[...]
    acc = None
    for j, b_ref in enumerate((b0_ref, b1_ref, b2_ref, b3_ref)):
        gate = _sigmoid(jnp.dot(h, wg_ref[j], preferred_element_type=F32))
        term = gate * jnp.dot(b_ref[...], wb_ref[j], preferred_element_type=F32)
        acc = term if acc is None else acc + term
    o_ref[...] = acc.astype(BF16)


def _merge(h, branches, wg, wb, layer, tm=512, tn=512):
    s, d = h.shape
    n = wg.shape[3]
    bw = wb.shape[2]
    tm = min(tm, s)
    b_spec = pl.BlockSpec((tm, bw), lambda j, i: (i, 0))
    return pl.pallas_call(
        _merge_kernel,
        grid=(n // tn, s // tm),
        in_specs=[pl.BlockSpec((tm, d), lambda j, i: (i, 0)),
                  b_spec, b_spec, b_spec, b_spec,
                  pl.BlockSpec((None, N_BRANCH, d, tn), lambda j, i: (layer, 0, 0, j)),
                  pl.BlockSpec((None, N_BRANCH, bw, tn), lambda j, i: (layer, 0, 0, j))],
        out_specs=pl.BlockSpec((tm, tn), lambda j, i: (i, j)),
        out_shape=jax.ShapeDtypeStruct((s, n), BF16),
        compiler_params=_cparams(2),
        name="merge",
    )(h, *branches, wg, wb)


def _attn_out_kernel(m_ref, w_ref, x_ref, gpost_ref, gnext_ref, xo_ref, ho_ref):
    y = jnp.dot(m_ref[...], w_ref[...], preferred_element_type=F32)
    xn = x_ref[...] + _rms(y, gpost_ref[...])
    xo_ref[...] = xn
    ho_ref[...] = _rms(xn, gnext_ref[...]).astype(BF16)


def _attn_out(merged, w, layer, x, gpost, gnext, tm=512):
    s, d = x.shape
    tm = min(tm, s)
    row = pl.BlockSpec((tm, d), lambda i: (i, 0))
    vec = pl.BlockSpec((1, d), lambda i: (0, 0))
    return pl.pallas_call(
        _attn_out_kernel,
        grid=(s // tm,),
        in_specs=[row, _resident((d, d), layer), row, vec, vec],
        out_specs=[row, row],
        out_shape=[jax.ShapeDtypeStruct((s, d), F32), jax.ShapeDtypeStruct((s, d), BF16)],
        compiler_params=_cparams(1),
        name="attn_out",
    )(merged, w, x, gpost.reshape(1, d), gnext.reshape(1, d))


def _ffn_in_kernel(h_ref, wg_ref, wu_ref, o_ref):
    h = h_ref[...]
    gate = jnp.dot(h, wg_ref[...], preferred_element_type=F32)
    up = jnp.dot(h, wu_ref[...], preferred_element_type=F32)
    o_ref[...] = (gate * _sigmoid(gate) * up).astype(BF16)


def _ffn_in(h, w, layer, tm=1024, tn=512):
    s, d = h.shape
    f = w.shape[2] // 2
    tm = min(tm, s)
    nt = f // tn
    return pl.pallas_call(
        _ffn_in_kernel,
        grid=(nt, s // tm),
        in_specs=[pl.BlockSpec((tm, d), lambda j, i: (i, 0)),
                  pl.BlockSpec((None, d, tn), lambda j, i: (layer, 0, j)),
                  pl.BlockSpec((None, d, tn), lambda j, i: (layer, 0, j + nt))],
        out_specs=pl.BlockSpec((tm, tn), lambda j, i: (i, j)),
        out_shape=jax.ShapeDtypeStruct((s, f), BF16),
        compiler_params=_cparams(2),
        name="ffn_in",
    )(h, w, w)


def _ffn_out_kernel(u_ref, w_ref, x_ref, g_ref, xo_ref, xb_ref):
    y = jnp.dot(u_ref[...], w_ref[...], preferred_element_type=F32)
    xn = x_ref[...] + _rms(y, g_ref[...])
    xo_ref[...] = xn
    xb_ref[...] = xn.astype(BF16)


def _ffn_out(u, w, layer, x, g, tm=256):
    s, d = x.shape
    f = u.shape[1]
    tm = min(tm, s)
    row = pl.BlockSpec((tm, d), lambda i: (i, 0))
    return pl.pallas_call(
        _ffn_out_kernel,
        grid=(s // tm,),
        in_specs=[pl.BlockSpec((tm, f), lambda i: (i, 0)),
                  _resident((f, d), layer),
                  row, pl.BlockSpec((1, d), lambda i: (0, 0))],
        out_specs=[row, row],
        out_shape=[jax.ShapeDtypeStruct((s, d), F32), jax.ShapeDtypeStruct((s, d), BF16)],
        compiler_params=_cparams(1),
        name="ffn_out",
    )(u, w, x, g.reshape(1, d))


def _ple_kernel(xb_ref, wg_ref, p_ref, wp_ref, x_ref, gpost_ref, *rest):
    gate = _sigmoid(jnp.dot(xb_ref[...], wg_ref[...], preferred_element_type=F32))
    emb = jnp.dot(p_ref[...].astype(BF16), wp_ref[...], preferred_element_type=F32)
    xn = x_ref[...] + _rms(gate * emb, gpost_ref[...])
    if len(rest) == 3:
        gnext_ref, xo_ref, ho_ref = rest
        ho_ref[...] = _rms(xn, gnext_ref[...]).astype(BF16)
    else:
        (xo_ref,) = rest
    xo_ref[...] = xn


def _ple(xb, wg, p, wp, layer, x, gpost, gnext, tm=512):
    s, d = x.shape
    pd = p.shape[3]
    tm = min(tm, s)
    row = pl.BlockSpec((tm, d), lambda i: (i, 0))
    vec = pl.BlockSpec((1, d), lambda i: (0, 0))
    in_specs = [row, _resident((d, d), layer),
                pl.BlockSpec((None, None, tm, pd), lambda i: (layer, 0, i, 0)),
                _resident((pd, d), layer), row, vec]
    args = [xb, wg, p, wp, x, gpost.reshape(1, d)]
    out_specs = [row]
    out_shape = [jax.ShapeDtypeStruct((s, d), F32)]
    if gnext is not None:
        in_specs.append(vec)
        args.append(gnext.reshape(1, d))
        out_specs.append(row)
        out_shape.append(jax.ShapeDtypeStruct((s, d), BF16))
    return pl.pallas_call(
        _ple_kernel,
        grid=(s // tm,),
        in_specs=in_specs,
        out_specs=out_specs,
        out_shape=out_shape,
        compiler_params=_cparams(1),
        name="ple",
    )(*args)


def _pad_cols(w, to):
    return jnp.pad(w, ((0, 0), (0, to - w.shape[1])))


def kernel(x, p, norm_mix_pre, norm_mix_post, norm_ffn_pre, norm_ffn_post, norm_ple_post, w_in, mla_qa_norm, mla_w_uq, mla_kva_norm, mla_w_ukv, diff_lambda, diff_subln, swa_sink, ax_q_norm, ax_k_norm, w_branch, w_branch_gate, w_o, w_ffn_in, w_ffn_out, w_ple, w_ple_gate):
    b, s, d = x.shape
    assert b == 1
    depth = w_in.shape[0]
    x2 = x.reshape(s, d)
    tab64, tab128, tab_ax = _position_tables(s)
    tk = min(FLASH_TK, s)

    c1 = MLA_Q_LORA + MLA_KV_LORA + MLA_ROPE

    t64, t128, tax = 0, 1, 2
    qs_b = DIFF_QK ** -0.5 * LOG2E
    qs_d = HEAD_DIM ** -0.5 * LOG2E
    plan = ([(t64, DIFF_QK // 2, None, qs_b, "zb", j) for j in range(4)]
            + [(t64, DIFF_QK // 2, None, 1.0, "zb", 4 + j) for j in range(4)]
            + [(None, 0, None, 1.0, "vtb", j) for j in range(4)]
            + [(t128, HEAD_DIM // 2, None, 1.0, "zc", j) for j in range(6)]
            + [(None, 0, None, 1.0, "zc", 6 + j) for j in range(2)]
            + [(tax, HEAD_DIM // 4, 0, qs_d, "zd", j) for j in range(4)]
            + [(tax, HEAD_DIM // 4, 1, 1.0, "zd", 4 + j) for j in range(2)]
            + [(None, 0, None, 1.0, "vtd", j) for j in range(2)])
    plan = tuple(plan)

    wg_all, wb_all = w_branch_gate.astype(BF16), w_branch.astype(BF16)
    wo_all = w_o.astype(BF16)
    wfi_all, wfo_all = w_ffn_in.astype(BF16), w_ffn_out.astype(BF16)
    wpg_all, wp_all = w_ple_gate.astype(BF16), w_ple.astype(BF16)

    h = _prenorm(x2, norm_mix_pre[0])
    for i in range(depth):
        lam_init = 0.8 - 0.6 * math.exp(-0.3 * i)
        wi = w_in[i].astype(BF16)
        w_mla = _pad_cols(wi[:, :c1], MLA_Q_LORA + MLA_KV_LORA + LANES)

        cq, ckv, kr = _mla_in(h, w_mla, tab64, mla_qa_norm[i], mla_kva_norm[i])
        wq = jnp.pad(mla_w_uq[i].reshape(MLA_Q_LORA, MLA_HEADS, MLA_NOPE + MLA_ROPE),
                     ((0, 0), (0, 0), (0, MLA_QK_PAD - MLA_NOPE - MLA_ROPE))
                     ).reshape(MLA_Q_LORA, MLA_HEADS * MLA_QK_PAD).astype(BF16)
        wkv = mla_w_ukv[i].reshape(MLA_KV_LORA, MLA_HEADS, MLA_NOPE + MLA_V)
        wk = wkv[:, :, :MLA_NOPE].reshape(MLA_KV_LORA, MLA_HEADS * MLA_NOPE).astype(BF16)
        wv = wkv[:, :, MLA_NOPE:].reshape(MLA_KV_LORA, MLA_HEADS * MLA_V).astype(BF16)
        qa, ka, vta = _mla_up(cq, ckv, kr, wq, wk, wv, tab64, tk)
        br_a = _flash(qa, ka, vta, mode="single",
                      n_steps=MLA_HEADS, q_blk=MLA_QK_PAD, q_off=0, k_blk=MLA_QK_PAD, k_off=0,
                      out_w=MLA_V, tq=1024)

        gains_ax = jnp.stack([ax_q_norm[i], ax_k_norm[i]])
        zb, vtb, zc, zd, vtd = _inproj(h, wi[:, c1:], (tab64, tab128, tab_ax), gains_ax, plan, tk)

        br_b = _flash(zb, zb, vtb, mode="diff",
                      n_steps=DIFF_HEADS, q_blk=HEAD_DIM, q_off=0, k_blk=HEAD_DIM,
                      k_off=DIFF_HEADS, out_w=HEAD_DIM,
                      lam=diff_lambda[i], subln=diff_subln[i], lam_init=lam_init)

        br_c = _swa(zc, swa_sink[i])

        group = AX_HEADS // AX_KV_HEADS
        br_d = _flash(zd, zd, vtd, mode="pair",
                      n_steps=AX_KV_HEADS, q_blk=group * HEAD_DIM, q_off=0, k_blk=HEAD_DIM,
                      k_off=AX_HEADS, out_w=group * HEAD_DIM)

        merged = _merge(h, (br_a, br_b, br_c, br_d), wg_all, wb_all, i)
        x2, h2 = _attn_out(merged, wo_all, i, x2, norm_mix_post[i], norm_ffn_pre[i])

        u = _ffn_in(h2, wfi_all, i)
        x2, xb = _ffn_out(u, wfo_all, i, x2, norm_ffn_post[i])

        g_next = norm_mix_pre[i + 1] if i + 1 < depth else None
        outs = _ple(xb, wpg_all, p, wp_all, i, x2, norm_ple_post[i], g_next)
        x2 = outs[0]
        h = outs[1] if g_next is not None else None
    return x2.reshape(b, s, d)
```

```python
import functools
import math

import jax
import jax.numpy as jnp
from jax import lax
from jax.experimental import pallas as pl
from jax.experimental.pallas import tpu as pltpu

HEAD_DIM = 128
ROPE_THETA = 10000.0
RMS_EPS = 1e-6
GRID_W = 64
MLA_HEADS = 4
MLA_Q_LORA = 512
MLA_KV_LORA = 512
MLA_NOPE = 128
MLA_ROPE = 64
MLA_V = 128
DIFF_HEADS = 4
DIFF_QK = HEAD_DIM // 2
SWA_HEADS = 4
SWA_KV_HEADS = 2
SWA_WINDOW = 128
AX_HEADS = 4
AX_KV_HEADS = 2
N_BRANCH = 4

LANES = 128
SUBLANES = 8
BF16_ROWS = 16
MLA_QK_PAD = 2 * LANES
LOG2E = 1.4426950408889634
MLA_QSCALE = (MLA_NOPE + MLA_ROPE) ** -0.5 * LOG2E
FLASH_TK = 1024
FLASH_PIECE = 256
FLASH_UNROLL = 8
VMEM_LIMIT = 56 * 1024 * 1024

BF16 = jnp.bfloat16
F32 = jnp.float32


def _cparams(n_grid):
    return pltpu.CompilerParams(
        dimension_semantics=("arbitrary",) * n_grid,
        vmem_limit_bytes=VMEM_LIMIT,
    )


def _resident(shape, layer=None):
    if layer is None:
        return pl.BlockSpec(shape, lambda *_: (0,) * len(shape), pipeline_mode=pl.Buffered(1))
    return pl.BlockSpec((None,) + tuple(shape), lambda *_: (layer,) + (0,) * len(shape),
                        pipeline_mode=pl.Buffered(1))


def _rms(y, g):
    return y * lax.rsqrt(jnp.mean(y * y, axis=-1, keepdims=True) + RMS_EPS) * g


def _rope(x, c, s1, s2, half):
    return (x * c + pltpu.roll(x, LANES - half, 1) * s1
            + pltpu.roll(x, half, 1) * s2)


def _sigmoid(x):
    return 1.0 / (1.0 + jnp.exp(-x))


def _position_tables(s):
    t = jnp.arange(s, dtype=F32)
    rows = s // GRID_W
    row_pos = jnp.broadcast_to(jnp.arange(rows)[:, None], (rows, GRID_W)).reshape(s).astype(F32)
    col_pos = jnp.broadcast_to(jnp.arange(GRID_W)[None, :], (rows, GRID_W)).reshape(s).astype(F32)
    inv64 = ROPE_THETA ** (-jnp.arange(0, 64, 2, dtype=F32) / 64)
    inv128 = ROPE_THETA ** (-jnp.arange(0, 128, 2, dtype=F32) / 128)
    lane = jnp.arange(LANES)
    first32 = (lane % 64) < 32
    first64 = lane < 64

    def tables(ang, first):
        c, sn = jnp.cos(ang), jnp.sin(ang)
        return (c, jnp.where(first[None, :], -sn, 0.0), jnp.where(first[None, :], 0.0, sn))

    ang64 = t[:, None] * inv64[lane % 32][None, :]
    ang128 = t[:, None] * inv128[lane % 64][None, :]
    pos_ax = jnp.where(first64[None, :], row_pos[:, None], col_pos[:, None])
    ang_ax = pos_ax * inv64[lane % 32][None, :]
    return tables(ang64, first32), tables(ang128, first64), tables(ang_ax, first32)


def _prenorm_kernel(x_ref, g_ref, o_ref):
    o_ref[...] = _rms(x_ref[...], g_ref[...]).astype(BF16)


def _prenorm(x, g, tm=512):
    s, d = x.shape
    tm = min(tm, s)
    return pl.pallas_call(
        _prenorm_kernel,
        grid=(s // tm,),
        in_specs=[pl.BlockSpec((tm, d), lambda i: (i, 0)),
                  pl.BlockSpec((1, d), lambda i: (0, 0))],
        out_specs=pl.BlockSpec((tm, d), lambda i: (i, 0)),
        out_shape=jax.ShapeDtypeStruct((s, d), BF16),
        compiler_params=_cparams(1),
        name="prenorm",
    )(x, g.reshape(1, d))


def _vt_spec(heads, dv, tm, tk):
    r = tk // tm
    return pl.BlockSpec((heads, None, dv, tm), lambda i: (0, i // r, 0, i % r))


def _inproj_kernel(h_ref, w_ref, *refs, plan):
    tab_refs, g_ref = refs[:9], refs[9]
    outs = dict(zip(("zb", "vtb", "zc", "zd", "vtd"), refs[10:]))
    acc = jnp.dot(h_ref[...], w_ref[...], preferred_element_type=F32)
    for b, (tab, half, gi, scale, dest, db) in enumerate(plan):
        blk = acc[:, b * LANES:(b + 1) * LANES]
        if gi is not None:
            blk = _rms(blk, g_ref[gi:gi + 1, :])
        if tab is not None:
            c_ref, s1_ref, s2_ref = tab_refs[3 * tab:3 * tab + 3]
            blk = _rope(blk, c_ref[...], s1_ref[...], s2_ref[...], half)
        if scale != 1.0:
            blk = blk * scale
        if dest in ("vtb", "vtd"):
            outs[dest][db] = blk.T.astype(BF16)
        else:
            outs[dest][:, db * LANES:(db + 1) * LANES] = blk.astype(BF16)


def _inproj(h, w, tabs, gains, plan, tk, tm=512):
    s, d = h.shape
    n = w.shape[1]
    tm = min(tm, s)
    nk = s // tk
    tab_spec = pl.BlockSpec((tm, LANES), lambda i: (i, 0))
    nb = 2 * DIFF_HEADS * HEAD_DIM
    nc = (SWA_HEADS + 2 * SWA_KV_HEADS) * HEAD_DIM
    nd = (AX_HEADS + AX_KV_HEADS) * HEAD_DIM
    row = lambda width: pl.BlockSpec((tm, width), lambda i: (i, 0))
    return pl.pallas_call(
        functools.partial(_inproj_kernel, plan=plan),
        grid=(s // tm,),
        in_specs=[pl.BlockSpec((tm, d), lambda i: (i, 0)), _resident((d, n))]
                 + [tab_spec] * 9 + [_resident(gains.shape)],
        out_specs=[row(nb), _vt_spec(DIFF_HEADS, HEAD_DIM, tm, tk), row(nc),
                   row(nd), _vt_spec(AX_KV_HEADS, HEAD_DIM, tm, tk)],
        out_shape=[jax.ShapeDtypeStruct((s, nb), BF16),
                   jax.ShapeDtypeStruct((DIFF_HEADS, nk, HEAD_DIM, tk), BF16),
                   jax.ShapeDtypeStruct((s, nc), BF16),
                   jax.ShapeDtypeStruct((s, nd), BF16),
                   jax.ShapeDtypeStruct((AX_KV_HEADS, nk, HEAD_DIM, tk), BF16)],
        compiler_params=_cparams(1),
        name="inproj",
    )(h, w, *[t for tab in tabs for t in tab], gains)


def _mla_proj_kernel(h_ref, w_ref, wq_ref, wk_ref, wv_ref, c_ref, s1_ref, s2_ref,
                     gq_ref, gkv_ref, q_ref, k_ref, vt_ref):
    c, s1, s2 = c_ref[...], s1_ref[...], s2_ref[...]
    acc = jnp.dot(h_ref[...], w_ref[...], preferred_element_type=F32)
    cq = _rms(acc[:, :MLA_Q_LORA], gq_ref[...]).astype(BF16)
    ckv = _rms(acc[:, MLA_Q_LORA:MLA_Q_LORA + MLA_KV_LORA], gkv_ref[...]).astype(BF16)
    kr = _rope(acc[:, MLA_Q_LORA + MLA_KV_LORA:], c, s1, s2, MLA_ROPE // 2).astype(BF16)
    q = jnp.dot(cq, wq_ref[...], preferred_element_type=F32)
    kn = jnp.dot(ckv, wk_ref[...], preferred_element_type=F32)
    for hd in range(MLA_HEADS):
        lo = hd * MLA_QK_PAD
        q_ref[:, lo:lo + LANES] = (q[:, lo:lo + LANES] * MLA_QSCALE).astype(BF16)
        q_ref[:, lo + LANES:lo + 2 * LANES] = (_rope(
            q[:, lo + LANES:lo + 2 * LANES], c, s1, s2, MLA_ROPE // 2) * MLA_QSCALE).astype(BF16)
        k_ref[:, lo:lo + LANES] = kn[:, hd * LANES:(hd + 1) * LANES].astype(BF16)
        k_ref[:, lo + LANES:lo + 2 * LANES] = kr
    v = jnp.dot(ckv, wv_ref[...], preferred_element_type=F32)
    for hd in range(MLA_HEADS):
        vt_ref[hd] = v[:, hd * MLA_V:(hd + 1) * MLA_V].T.astype(BF16)


def _mla_proj(h, w, wq, wk, wv, tabs, gq, gkv, tk, tm=512):
    s, d = h.shape
    tm = min(tm, s)
    c, s1, s2 = tabs
    nq = MLA_HEADS * MLA_QK_PAD
    tab_spec = pl.BlockSpec((tm, LANES), lambda i: (i, 0))
    return pl.pallas_call(
        _mla_proj_kernel,
        grid=(s // tm,),
        in_specs=[pl.BlockSpec((tm, d), lambda i: (i, 0)),
                  _resident(w.shape), _resident(wq.shape), _resident(wk.shape),
                  _resident(wv.shape), tab_spec, tab_spec, tab_spec,
                  _resident((1, MLA_Q_LORA)), _resident((1, MLA_KV_LORA))],
        out_specs=[pl.BlockSpec((tm, nq), lambda i: (i, 0)),
                   pl.BlockSpec((tm, nq), lambda i: (i, 0)),
                   _vt_spec(MLA_HEADS, MLA_V, tm, tk)],
        out_shape=[jax.ShapeDtypeStruct((s, nq), BF16),
                   jax.ShapeDtypeStruct((s, nq), BF16),
                   jax.ShapeDtypeStruct((MLA_HEADS, s // tk, MLA_V, tk), BF16)],
        compiler_params=_cparams(1),
        name="mla_proj",
    )(h, w, wq, wk, wv, c, s1, s2, gq.reshape(1, -1), gkv.reshape(1, -1))


def _flash_kernel(*refs, mode, tq, tk, unroll, lam_init):
    if mode == "diff":
        (q_ref, k_ref, vt_ref, lam_ref, subln_ref, o_ref,
         qcat_ref, s0_ref, s1_ref, p_ref, acc_ref) = refs
    else:
        (q_ref, k_ref, vt_ref, o_ref,
         qcat_ref, s0_ref, s1_ref, p_ref, acc_ref) = refs
    qt = q_ref[...].astype(F32).T
    if mode == "single":
        qcat_ref[...] = qt.astype(BF16)
    elif mode == "pair":
        qcat_ref[:, :tq] = qt[:LANES].astype(BF16)
        qcat_ref[:, tq:] = qt[LANES:].astype(BF16)
    else:
        row = lax.broadcasted_iota(jnp.int32, qt.shape, 0)
        qcat_ref[:, :tq] = jnp.where(row < DIFF_QK, qt, 0.0).astype(BF16)
        qcat_ref[:, tq:] = jnp.where(row >= DIFF_QK, qt, 0.0).astype(BF16)
    ncols = qcat_ref.shape[1]
    dv = vt_ref.shape[1]
    nk = vt_ref.shape[0]
    s_bufs = (s0_ref, s1_ref)
    n_pieces = tk // FLASH_PIECE

    def scores(ci, r, dst_ref):
        k0 = pl.multiple_of(ci * tk + r * FLASH_PIECE, FLASH_PIECE)
        sc = jnp.dot(k_ref[pl.ds(k0, FLASH_PIECE), :], qcat_ref[...],
                     preferred_element_type=F32)
        dst_ref[r * FLASH_PIECE:(r + 1) * FLASH_PIECE, :] = sc
        return jnp.max(sc, axis=0, keepdims=True)

    def probs(s_ref, r, mb, psum):
        for rr in range(FLASH_PIECE // BF16_ROWS):
            lo = r * FLASH_PIECE + rr * BF16_ROWS
            p0 = jnp.exp2(s_ref[lo:lo + SUBLANES, :] - mb)
            p1 = jnp.exp2(s_ref[lo + SUBLANES:lo + BF16_ROWS, :] - mb)
            psum = psum + p0 + p1
            p_ref[lo:lo + BF16_ROWS, :] = jnp.concatenate([p0, p1], axis=0).astype(BF16)
        return psum

    def pv_piece(ci, r):
        lo = r * FLASH_PIECE
        return jnp.dot(vt_ref.at[ci][:, lo:lo + FLASH_PIECE], p_ref[lo:lo + FLASH_PIECE, :],
                       preferred_element_type=F32)

    def body(j, carry):
        m, l, cmax = carry
        for b in range(unroll):
            ci = unroll * j + b
            nxt = jnp.minimum(ci + 1, nk - 1)
            m_new = jnp.maximum(m, cmax)
            alpha = jnp.exp2(m - m_new)
            mb = jnp.broadcast_to(m_new, (SUBLANES, ncols))
            psum = jnp.zeros((SUBLANES, ncols), F32)
            pv = None
            for r in range(n_pieces):
                cm = scores(nxt, r, s_bufs[(b + 1) % 2])
                cmax_next = cm if r == 0 else jnp.maximum(cmax_next, cm)
                psum = probs(s_bufs[b % 2], r, mb, psum)
                d = pv_piece(ci, r)
                pv = d if pv is None else pv + d
            acc_ref[...] = alpha * acc_ref[...] + pv
            l = alpha * l + jnp.sum(psum, axis=0, keepdims=True)
            m, cmax = m_new, cmax_next
        return m, l, cmax

    cmax0 = scores(0, 0, s_bufs[0])
    for r in range(1, n_pieces):
        cmax0 = jnp.maximum(cmax0, scores(0, r, s_bufs[0]))
    acc_ref[...] = jnp.zeros_like(acc_ref)
    init = (jnp.full((1, ncols), -jnp.inf, F32), jnp.zeros((1, ncols), F32), cmax0)
    _, l, _ = lax.fori_loop(0, nk // unroll, body, init)
    ot = acc_ref[...] * (1.0 / l)
    if mode == "single":
        o_ref[...] = ot.T.astype(BF16)
    elif mode == "pair":
        o_ref[:, :dv] = ot[:, :tq].T.astype(BF16)
        o_ref[:, dv:] = ot[:, tq:].T.astype(BF16)
    else:
        lp = lam_ref[...]
        lam = (jnp.exp(jnp.sum(lp[0:1] * lp[1:2], axis=1, keepdims=True))
               - jnp.exp(jnp.sum(lp[2:3] * lp[3:4], axis=1, keepdims=True)) + lam_init)
        d = (ot[:, :tq] - lam * ot[:, tq:]).T
        o_ref[...] = (_rms(d, subln_ref[...]) * (1.0 - lam_init)).astype(BF16)


def _flash(q_arr, k_arr, vt, *, mode, n_steps, q_blk, q_off, k_blk, k_off, out_w,
           tq=512, lam=None, subln=None, lam_init=0.0):
    s = q_arr.shape[0]
    tq = min(tq, s)
    _, nk, dv, tk = vt.shape
    unroll = math.gcd(nk, FLASH_UNROLL)
    assert unroll % 2 == 0
    ncols = tq if mode == "single" else 2 * tq
    in_specs = [pl.BlockSpec((tq, q_blk), lambda h, i: (i, h + q_off)),
                pl.BlockSpec((s, k_blk), lambda h, i: (0, h + k_off)),
                pl.BlockSpec((None, nk, dv, tk), lambda h, i: (h, 0, 0, 0))]
    args = [q_arr, k_arr, vt]
    if mode == "diff":
        in_specs += [pl.BlockSpec(lam.shape, lambda h, i: (0, 0)),
                     pl.BlockSpec((1, HEAD_DIM), lambda h, i: (0, 0))]
        args += [lam, subln.reshape(1, HEAD_DIM)]
    return pl.pallas_call(
        functools.partial(_flash_kernel, mode=mode, tq=tq, tk=tk, unroll=unroll,
                          lam_init=lam_init),
        grid=(n_steps, s // tq),
        in_specs=in_specs,
        out_specs=pl.BlockSpec((tq, out_w), lambda h, i: (i, h)),
        out_shape=jax.ShapeDtypeStruct((s, n_steps * out_w), BF16),
        scratch_shapes=[pltpu.VMEM((k_blk, ncols), BF16),
                        pltpu.VMEM((tk, ncols), F32), pltpu.VMEM((tk, ncols), F32),
                        pltpu.VMEM((tk, ncols), BF16),
                        pltpu.VMEM((dv, ncols), F32)],
        compiler_params=_cparams(2),
        name="flash_" + mode,
    )(*args)


def _swa_kernel(sink_ref, q_ref, k_ref, v_ref, o_ref, *, tq, win, scale):
    g = pl.program_id(0)
    i = pl.program_id(1)
    s = k_ref.shape[0]
    q0 = i * tq
    k0 = pl.multiple_of(jnp.clip(q0 - SWA_WINDOW, 0, s - win), LANES)
    kw = k_ref[pl.ds(k0, win), :]
    vw = v_ref[pl.ds(k0, win), :]
    qpos = q0 + lax.broadcasted_iota(jnp.int32, (tq, win), 0)
    kpos = k0 + lax.broadcasted_iota(jnp.int32, (tq, win), 1)
    valid = jnp.abs(kpos - qpos) <= SWA_WINDOW
    group = SWA_HEADS // SWA_KV_HEADS
    for j in range(group):
        q = q_ref[:, j * HEAD_DIM:(j + 1) * HEAD_DIM]
        sc = lax.dot_general(q, kw, (((1,), (1,)), ((), ())),
                             preferred_element_type=F32) * scale
        sc = jnp.where(valid, sc, -jnp.inf)
        sink = sink_ref[g * group + j]
        m = jnp.maximum(jnp.max(sc, axis=1, keepdims=True), sink)
        p = jnp.exp(sc - m)
        denom = jnp.sum(p, axis=1, keepdims=True) + jnp.exp(sink - m)
        pr = (p * (1.0 / denom)).astype(BF16)
        o_ref[:, j * HEAD_DIM:(j + 1) * HEAD_DIM] = jnp.dot(
            pr, vw, preferred_element_type=F32).astype(BF16)


def _swa(z, sink, tq=512):
    s = z.shape[0]
    tq = min(tq, s)
    win = min(tq + 2 * SWA_WINDOW, s)
    group = SWA_HEADS // SWA_KV_HEADS
    qw = group * HEAD_DIM
    k_off = SWA_HEADS
    v_off = SWA_HEADS + SWA_KV_HEADS
    return pl.pallas_call(
        functools.partial(_swa_kernel, tq=tq, win=win, scale=HEAD_DIM ** -0.5),
        grid=(SWA_KV_HEADS, s // tq),
        in_specs=[pl.BlockSpec(memory_space=pltpu.SMEM),
                  pl.BlockSpec((tq, qw), lambda g, i: (i, g)),
                  pl.BlockSpec((s, HEAD_DIM), lambda g, i: (0, g + k_off)),
                  pl.BlockSpec((s, HEAD_DIM), lambda g, i: (0, g + v_off))],
        out_specs=pl.BlockSpec((tq, qw), lambda g, i: (i, g)),
        out_shape=jax.ShapeDtypeStruct((s, SWA_HEADS * HEAD_DIM), BF16),
        compiler_params=_cparams(2),
        name="swa",
    )(sink, z, z, z)


def _merge_kernel(h_ref, b0_ref, b1_ref, b2_ref, b3_ref, wg_ref, wb_ref, o_ref):
    h = h_ref[...]
    acc = None
    for j, b_ref in enumerate((b0_ref, b1_ref, b2_ref, b3_ref)):
        gate = _sigmoid(jnp.dot(h, wg_ref[j], preferred_element_type=F32))
        term = gate * jnp.dot(b_ref[...], wb_ref[j], preferred_element_type=F32)
        acc = term if acc is None else acc + term
    o_ref[...] = acc.astype(BF16)


def _merge(h, branches, wg, wb, layer, tm=512, tn=512):
    s, d = h.shape
    n = wg.shape[3]
    bw = wb.shape[2]
    tm = min(tm, s)
    b_spec = pl.BlockSpec((tm, bw), lambda j, i: (i, 0))
    return pl.pallas_call(
        _merge_kernel,
        grid=(n // tn, s // tm),
        in_specs=[pl.BlockSpec((tm, d), lambda j, i: (i, 0)),
                  b_spec, b_spec, b_spec, b_spec,
                  pl.BlockSpec((None, N_BRANCH, d, tn), lambda j, i: (layer, 0, 0, j)),
                  pl.BlockSpec((None, N_BRANCH, bw, tn), lambda j, i: (layer, 0, 0, j))],
        out_specs=pl.BlockSpec((tm, tn), lambda j, i: (i, j)),
        out_shape=jax.ShapeDtypeStruct((s, n), BF16),
        compiler_params=_cparams(2),
        name="merge",
    )(h, *branches, wg, wb)


def _attn_out_kernel(m_ref, w_ref, x_ref, gpost_ref, gnext_ref, xo_ref, ho_ref):
    y = jnp.dot(m_ref[...], w_ref[...], preferred_element_type=F32)
    xn = x_ref[...] + _rms(y, gpost_ref[...])
    xo_ref[...] = xn
    ho_ref[...] = _rms(xn, gnext_ref[...]).astype(BF16)


def _attn_out(merged, w, layer, x, gpost, gnext, tm=512):
    s, d = x.shape
    tm = min(tm, s)
    row = pl.BlockSpec((tm, d), lambda i: (i, 0))
    vec = pl.BlockSpec((1, d), lambda i: (0, 0))
    return pl.pallas_call(
        _attn_out_kernel,
        grid=(s // tm,),
        in_specs=[row, _resident((d, d), layer), row, vec, vec],
        out_specs=[row, row],
        out_shape=[jax.ShapeDtypeStruct((s, d), F32), jax.ShapeDtypeStruct((s, d), BF16)],
        compiler_params=_cparams(1),
        name="attn_out",
    )(merged, w, x, gpost.reshape(1, d), gnext.reshape(1, d))


def _ffn_in_kernel(h_ref, wg_ref, wu_ref, o_ref):
    h = h_ref[...]
    gate = jnp.dot(h, wg_ref[...], preferred_element_type=F32)
    up = jnp.dot(h, wu_ref[...], preferred_element_type=F32)
    o_ref[...] = (gate * _sigmoid(gate) * up).astype(BF16)


def _ffn_in(h, w, layer, tm=1024, tn=512):
    s, d = h.shape
    f = w.shape[2] // 2
    tm = min(tm, s)
    nt = f // tn
    return pl.pallas_call(
        _ffn_in_kernel,
        grid=(nt, s // tm),
        in_specs=[pl.BlockSpec((tm, d), lambda j, i: (i, 0)),
                  pl.BlockSpec((None, d, tn), lambda j, i: (layer, 0, j)),
                  pl.BlockSpec((None, d, tn), lambda j, i: (layer, 0, j + nt))],
        out_specs=pl.BlockSpec((tm, tn), lambda j, i: (i, j)),
        out_shape=jax.ShapeDtypeStruct((s, f), BF16),
        compiler_params=_cparams(2),
        name="ffn_in",
    )(h, w, w)


def _ffn_out_kernel(u_ref, w_ref, x_ref, g_ref, xo_ref, xb_ref):
    y = jnp.dot(u_ref[...], w_ref[...], preferred_element_type=F32)
    xn = x_ref[...] + _rms(y, g_ref[...])
    xo_ref[...] = xn
    xb_ref[...] = xn.astype(BF16)


def _ffn_out(u, w, layer, x, g, tm=256):
    s, d = x.shape
    f = u.shape[1]
    tm = min(tm, s)
    row = pl.BlockSpec((tm, d), lambda i: (i, 0))
    return pl.pallas_call(
        _ffn_out_kernel,
        grid=(s // tm,),
        in_specs=[pl.BlockSpec((tm, f), lambda i: (i, 0)),
                  _resident((f, d), layer),
                  row, pl.BlockSpec((1, d), lambda i: (0, 0))],
        out_specs=[row, row],
        out_shape=[jax.ShapeDtypeStruct((s, d), F32), jax.ShapeDtypeStruct((s, d), BF16)],
        compiler_params=_cparams(1),
        name="ffn_out",
    )(u, w, x, g.reshape(1, d))


def _ple_kernel(xb_ref, wg_ref, p_ref, wp_ref, x_ref, gpost_ref, *rest):
    gate = _sigmoid(jnp.dot(xb_ref[...], wg_ref[...], preferred_element_type=F32))
    emb = jnp.dot(p_ref[...].astype(BF16), wp_ref[...], preferred_element_type=F32)
    xn = x_ref[...] + _rms(gate * emb, gpost_ref[...])
    if len(rest) == 3:
        gnext_ref, xo_ref, ho_ref = rest
        ho_ref[...] = _rms(xn, gnext_ref[...]).astype(BF16)
    else:
        (xo_ref,) = rest
    xo_ref[...] = xn


def _ple(xb, wg, p, wp, layer, x, gpost, gnext, tm=512):
    s, d = x.shape
    pd = p.shape[3]
    tm = min(tm, s)
    row = pl.BlockSpec((tm, d), lambda i: (i, 0))
    vec = pl.BlockSpec((1, d), lambda i: (0, 0))
    in_specs = [row, _resident((d, d), layer),
                pl.BlockSpec((None, None, tm, pd), lambda i: (layer, 0, i, 0)),
                _resident((pd, d), layer), row, vec]
    args = [xb, wg, p, wp, x, gpost.reshape(1, d)]
    out_specs = [row]
    out_shape = [jax.ShapeDtypeStruct((s, d), F32)]
    if gnext is not None:
        in_specs.append(vec)
        args.append(gnext.reshape(1, d))
        out_specs.append(row)
        out_shape.append(jax.ShapeDtypeStruct((s, d), BF16))
    return pl.pallas_call(
        _ple_kernel,
        grid=(s // tm,),
        in_specs=in_specs,
        out_specs=out_specs,
        out_shape=out_shape,
        compiler_params=_cparams(1),
        name="ple",
    )(*args)


def _pad_cols(w, to):
    return jnp.pad(w, ((0, 0), (0, to - w.shape[1])))


def kernel(x, p, norm_mix_pre, norm_mix_post, norm_ffn_pre, norm_ffn_post, norm_ple_post, w_in, mla_qa_norm, mla_w_uq, mla_kva_norm, mla_w_ukv, diff_lambda, diff_subln, swa_sink, ax_q_norm, ax_k_norm, w_branch, w_branch_gate, w_o, w_ffn_in, w_ffn_out, w_ple, w_ple_gate):
    b, s, d = x.shape
    assert b == 1
    depth = w_in.shape[0]
    x2 = x.reshape(s, d)
    tab64, tab128, tab_ax = _position_tables(s)
    tk = min(FLASH_TK, s)

    c1 = MLA_Q_LORA + MLA_KV_LORA + MLA_ROPE

    t64, t128, tax = 0, 1, 2
    qs_b = DIFF_QK ** -0.5 * LOG2E
    qs_d = HEAD_DIM ** -0.5 * LOG2E
    plan = ([(t64, DIFF_QK // 2, None, qs_b, "zb", j) for j in range(4)]
            + [(t64, DIFF_QK // 2, None, 1.0, "zb", 4 + j) for j in range(4)]
            + [(None, 0, None, 1.0, "vtb", j) for j in range(4)]
            + [(t128, HEAD_DIM // 2, None, 1.0, "zc", j) for j in range(6)]
            + [(None, 0, None, 1.0, "zc", 6 + j) for j in range(2)]
            + [(tax, HEAD_DIM // 4, 0, qs_d, "zd", j) for j in range(4)]
            + [(tax, HEAD_DIM // 4, 1, 1.0, "zd", 4 + j) for j in range(2)]
            + [(None, 0, None, 1.0, "vtd", j) for j in range(2)])
    plan = tuple(plan)

    wg_all, wb_all = w_branch_gate.astype(BF16), w_branch.astype(BF16)
    wo_all = w_o.astype(BF16)
    wfi_all, wfo_all = w_ffn_in.astype(BF16), w_ffn_out.astype(BF16)
    wpg_all, wp_all = w_ple_gate.astype(BF16), w_ple.astype(BF16)

    h = _prenorm(x2, norm_mix_pre[0])
    for i in range(depth):
        lam_init = 0.8 - 0.6 * math.exp(-0.3 * i)
        wi = w_in[i].astype(BF16)
        w_mla = _pad_cols(wi[:, :c1], MLA_Q_LORA + MLA_KV_LORA + LANES)

        wq = jnp.pad(mla_w_uq[i].reshape(MLA_Q_LORA, MLA_HEADS, MLA_NOPE + MLA_ROPE),
                     ((0, 0), (0, 0), (0, MLA_QK_PAD - MLA_NOPE - MLA_ROPE))
                     ).reshape(MLA_Q_LORA, MLA_HEADS * MLA_QK_PAD).astype(BF16)
        wkv = mla_w_ukv[i].reshape(MLA_KV_LORA, MLA_HEADS, MLA_NOPE + MLA_V)
        wk = wkv[:, :, :MLA_NOPE].reshape(MLA_KV_LORA, MLA_HEADS * MLA_NOPE).astype(BF16)
        wv = wkv[:, :, MLA_NOPE:].reshape(MLA_KV_LORA, MLA_HEADS * MLA_V).astype(BF16)
        qa, ka, vta = _mla_proj(h, w_mla, wq, wk, wv, tab64, mla_qa_norm[i], mla_kva_norm[i], tk)
        br_a = _flash(qa, ka, vta, mode="single",
                      n_steps=MLA_HEADS, q_blk=MLA_QK_PAD, q_off=0, k_blk=MLA_QK_PAD, k_off=0,
                      out_w=MLA_V, tq=1024)

        gains_ax = jnp.stack([ax_q_norm[i], ax_k_norm[i]])
        zb, vtb, zc, zd, vtd = _inproj(h, wi[:, c1:], (tab64, tab128, tab_ax), gains_ax, plan, tk)

        br_b = _flash(zb, zb, vtb, mode="diff",
                      n_steps=DIFF_HEADS, q_blk=HEAD_DIM, q_off=0, k_blk=HEAD_DIM,
                      k_off=DIFF_HEADS, out_w=HEAD_DIM,
                      lam=diff_lambda[i], subln=diff_subln[i], lam_init=lam_init)

        br_c = _swa(zc, swa_sink[i])

        group = AX_HEADS // AX_KV_HEADS
        br_d = _flash(zd, zd, vtd, mode="pair",
                      n_steps=AX_KV_HEADS, q_blk=group * HEAD_DIM, q_off=0, k_blk=HEAD_DIM,
                      k_off=AX_HEADS, out_w=group * HEAD_DIM)

        merged = _merge(h, (br_a, br_b, br_c, br_d), wg_all, wb_all, i)
        x2, h2 = _attn_out(merged, wo_all, i, x2, norm_mix_post[i], norm_ffn_pre[i])

        u = _ffn_in(h2, wfi_all, i)
        x2, xb = _ffn_out(u, wfo_all, i, x2, norm_ffn_post[i])

        g_next = norm_mix_pre[i + 1] if i + 1 < depth else None
        outs = _ple(xb, wpg_all, p, wp_all, i, x2, norm_ple_post[i], g_next)
        x2 = outs[0]
        h = outs[1] if g_next is not None else None
    return x2.reshape(b, s, d)
```

```python
import functools
import math

import jax
import jax.numpy as jnp
from jax import lax
from jax.experimental import pallas as pl
from jax.experimental.pallas import tpu as pltpu

HEAD_DIM = 128
ROPE_THETA = 10000.0
RMS_EPS = 1e-6
GRID_W = 64
MLA_HEADS = 4
MLA_Q_LORA = 512
MLA_KV_LORA = 512
MLA_NOPE = 128
MLA_ROPE = 64
MLA_V = 128
DIFF_HEADS = 4
DIFF_QK = HEAD_DIM // 2
SWA_HEADS = 4
SWA_KV_HEADS = 2
SWA_WINDOW = 128
AX_HEADS = 4
AX_KV_HEADS = 2
N_BRANCH = 4

LANES = 128
SUBLANES = 8
BF16_ROWS = 16
MLA_QK_PAD = 2 * LANES
LOG2E = 1.4426950408889634
MLA_QSCALE = (MLA_NOPE + MLA_ROPE) ** -0.5 * LOG2E
FLASH_TK = 1024
FLASH_PIECE = 256
FLASH_UNROLL = 8
VMEM_LIMIT = 56 * 1024 * 1024

BF16 = jnp.bfloat16
F32 = jnp.float32


def _cparams(n_grid):
    return pltpu.CompilerParams(
        dimension_semantics=("arbitrary",) * n_grid,
        vmem_limit_bytes=VMEM_LIMIT,
    )


def _resident(shape, layer=None):
    if layer is None:
        return pl.BlockSpec(shape, lambda *_: (0,) * len(shape), pipeline_mode=pl.Buffered(1))
    return pl.BlockSpec((None,) + tuple(shape), lambda *_: (layer,) + (0,) * len(shape),
                        pipeline_mode=pl.Buffered(1))


def _rms(y, g):
    return y * lax.rsqrt(jnp.mean(y * y, axis=-1, keepdims=True) + RMS_EPS) * g


def _rope(x, c, s1, s2, half):
    return (x * c + pltpu.roll(x, LANES - half, 1) * s1
            + pltpu.roll(x, half, 1) * s2)


def _sigmoid(x):
    return 1.0 / (1.0 + jnp.exp(-x))


def _position_tables(s):
    t = jnp.arange(s, dtype=F32)
    rows = s // GRID_W
    row_pos = jnp.broadcast_to(jnp.arange(rows)[:, None], (rows, GRID_W)).reshape(s).astype(F32)
    col_pos = jnp.broadcast_to(jnp.arange(GRID_W)[None, :], (rows, GRID_W)).reshape(s).astype(F32)
    inv64 = ROPE_THETA ** (-jnp.arange(0, 64, 2, dtype=F32) / 64)
    inv128 = ROPE_THETA ** (-jnp.arange(0, 128, 2, dtype=F32) / 128)
    lane = jnp.arange(LANES)
    first32 = (lane % 64) < 32
    first64 = lane < 64

    def tables(ang, first):
        c, sn = jnp.cos(ang), jnp.sin(ang)
        return (c, jnp.where(first[None, :], -sn, 0.0), jnp.where(first[None, :], 0.0, sn))

    ang64 = t[:, None] * inv64[lane % 32][None, :]
    ang128 = t[:, None] * inv128[lane % 64][None, :]
    pos_ax = jnp.where(first64[None, :], row_pos[:, None], col_pos[:, None])
    ang_ax = pos_ax * inv64[lane % 32][None, :]
    return tables(ang64, first32), tables(ang128, first64), tables(ang_ax, first32)


def _prenorm_kernel(x_ref, g_ref, o_ref):
    o_ref[...] = _rms(x_ref[...], g_ref[...]).astype(BF16)


def _prenorm(x, g, tm=512):
    s, d = x.shape
    tm = min(tm, s)
    return pl.pallas_call(
        _prenorm_kernel,
        grid=(s // tm,),
        in_specs=[pl.BlockSpec((tm, d), lambda i: (i, 0)),
                  pl.BlockSpec((1, d), lambda i: (0, 0))],
        out_specs=pl.BlockSpec((tm, d), lambda i: (i, 0)),
        out_shape=jax.ShapeDtypeStruct((s, d), BF16),
        compiler_params=_cparams(1),
        name="prenorm",
    )(x, g.reshape(1, d))


def _vt_spec(heads, dv, tm, tk):
    r = tk // tm
    return pl.BlockSpec((heads, None, dv, tm), lambda i: (0, i // r, 0, i % r))


def _inproj_kernel(h_ref, w_ref, *refs, plan):
    tab_refs, g_ref = refs[:9], refs[9]
    outs = dict(zip(("zb", "vtb", "zc", "zd", "vtd"), refs[10:]))
    acc = jnp.dot(h_ref[...], w_ref[...], preferred_element_type=F32)
    for b, (tab, half, gi, scale, dest, db) in enumerate(plan):
        blk = acc[:, b * LANES:(b + 1) * LANES]
        if gi is not None:
            blk = _rms(blk, g_ref[gi:gi + 1, :])
        if tab is not None:
            c_ref, s1_ref, s2_ref = tab_refs[3 * tab:3 * tab + 3]
            blk = _rope(blk, c_ref[...], s1_ref[...], s2_ref[...], half)
        if scale != 1.0:
            blk = blk * scale
        if dest in ("vtb", "vtd"):
            outs[dest][db] = blk.T.astype(BF16)
        else:
            outs[dest][:, db * LANES:(db + 1) * LANES] = blk.astype(BF16)


def _inproj(h, w, tabs, gains, plan, tk, tm=512):
    s, d = h.shape
    n = w.shape[1]
    tm = min(tm, s)
    nk = s // tk
    tab_spec = pl.BlockSpec((tm, LANES), lambda i: (i, 0))
    nb = 2 * DIFF_HEADS * HEAD_DIM
    nc = (SWA_HEADS + 2 * SWA_KV_HEADS) * HEAD_DIM
    nd = (AX_HEADS + AX_KV_HEADS) * HEAD_DIM
    row = lambda width: pl.BlockSpec((tm, width), lambda i: (i, 0))
    return pl.pallas_call(
        functools.partial(_inproj_kernel, plan=plan),
        grid=(s // tm,),
        in_specs=[pl.BlockSpec((tm, d), lambda i: (i, 0)), _resident((d, n))]
                 + [tab_spec] * 9 + [_resident(gains.shape)],
        out_specs=[row(nb), _vt_spec(DIFF_HEADS, HEAD_DIM, tm, tk), row(nc),
                   row(nd), _vt_spec(AX_KV_HEADS, HEAD_DIM, tm, tk)],
        out_shape=[jax.ShapeDtypeStruct((s, nb), BF16),
                   jax.ShapeDtypeStruct((DIFF_HEADS, nk, HEAD_DIM, tk), BF16),
                   jax.ShapeDtypeStruct((s, nc), BF16),
                   jax.ShapeDtypeStruct((s, nd), BF16),
                   jax.ShapeDtypeStruct((AX_KV_HEADS, nk, HEAD_DIM, tk), BF16)],
        compiler_params=_cparams(1),
        name="inproj",
    )(h, w, *[t for tab in tabs for t in tab], gains)


def _mla_proj_kernel(h_ref, w_ref, wq_ref, wk_ref, wv_ref, c_ref, s1_ref, s2_ref,
                     gq_ref, gkv_ref, q_ref, k_ref, vt_ref):
    c, s1, s2 = c_ref[...], s1_ref[...], s2_ref[...]
    acc = jnp.dot(h_ref[...], w_ref[...], preferred_element_type=F32)
    cq = _rms(acc[:, :MLA_Q_LORA], gq_ref[...]).astype(BF16)
    ckv = _rms(acc[:, MLA_Q_LORA:MLA_Q_LORA + MLA_KV_LORA], gkv_ref[...]).astype(BF16)
    kr = _rope(acc[:, MLA_Q_LORA + MLA_KV_LORA:], c, s1, s2, MLA_ROPE // 2).astype(BF16)
    q = jnp.dot(cq, wq_ref[...], preferred_element_type=F32)
    kn = jnp.dot(ckv, wk_ref[...], preferred_element_type=F32)
    for hd in range(MLA_HEADS):
        lo = hd * MLA_QK_PAD
        q_ref[:, lo:lo + LANES] = (q[:, lo:lo + LANES] * MLA_QSCALE).astype(BF16)
        q_ref[:, lo + LANES:lo + 2 * LANES] = (_rope(
            q[:, lo + LANES:lo + 2 * LANES], c, s1, s2, MLA_ROPE // 2) * MLA_QSCALE).astype(BF16)
        k_ref[:, lo:lo + LANES] = kn[:, hd * LANES:(hd + 1) * LANES].astype(BF16)
        k_ref[:, lo + LANES:lo + 2 * LANES] = kr
    v = jnp.dot(ckv, wv_ref[...], preferred_element_type=F32)
    for hd in range(MLA_HEADS):
        vt_ref[hd] = v[:, hd * MLA_V:(hd + 1) * MLA_V].T.astype(BF16)


def _mla_proj(h, w, wq, wk, wv, tabs, gq, gkv, tk, tm=512):
    s, d = h.shape
    tm = min(tm, s)
    c, s1, s2 = tabs
    nq = MLA_HEADS * MLA_QK_PAD
    tab_spec = pl.BlockSpec((tm, LANES), lambda i: (i, 0))
    return pl.pallas_call(
        _mla_proj_kernel,
        grid=(s // tm,),
        in_specs=[pl.BlockSpec((tm, d), lambda i: (i, 0)),
                  _resident(w.shape), _resident(wq.shape), _resident(wk.shape),
                  _resident(wv.shape), tab_spec, tab_spec, tab_spec,
                  _resident((1, MLA_Q_LORA)), _resident((1, MLA_KV_LORA))],
        out_specs=[pl.BlockSpec((tm, nq), lambda i: (i, 0)),
                   pl.BlockSpec((tm, nq), lambda i: (i, 0)),
                   _vt_spec(MLA_HEADS, MLA_V, tm, tk)],
        out_shape=[jax.ShapeDtypeStruct((s, nq), BF16),
                   jax.ShapeDtypeStruct((s, nq), BF16),
                   jax.ShapeDtypeStruct((MLA_HEADS, s // tk, MLA_V, tk), BF16)],
        compiler_params=_cparams(1),
        name="mla_proj",
    )(h, w, wq, wk, wv, c, s1, s2, gq.reshape(1, -1), gkv.reshape(1, -1))


def _flash_kernel(*refs, mode, tq, tk, unroll, lam_init):
    if mode == "diff":
        (q_ref, k_ref, vt_ref, lam_ref, subln_ref, o_ref,
         qcat_ref, s0_ref, s1_ref, p_ref, acc_ref) = refs
    else:
        (q_ref, k_ref, vt_ref, o_ref,
         qcat_ref, s0_ref, s1_ref, p_ref, acc_ref) = refs
    qt = q_ref[...].astype(F32).T
    if mode == "single":
        qcat_ref[...] = qt.astype(BF16)
    elif mode == "pair":
        qcat_ref[:, :tq] = qt[:LANES].astype(BF16)
        qcat_ref[:, tq:] = qt[LANES:].astype(BF16)
    else:
        row = lax.broadcasted_iota(jnp.int32, qt.shape, 0)
        qcat_ref[:, :tq] = jnp.where(row < DIFF_QK, qt, 0.0).astype(BF16)
        qcat_ref[:, tq:] = jnp.where(row >= DIFF_QK, qt, 0.0).astype(BF16)
    ncols = qcat_ref.shape[1]
    dv = vt_ref.shape[1]
    nk = vt_ref.shape[0]
    s_bufs = (s0_ref, s1_ref)
    n_pieces = tk // FLASH_PIECE

    def scores(ci, r, dst_ref):
        k0 = pl.multiple_of(ci * tk + r * FLASH_PIECE, FLASH_PIECE)
        sc = jnp.dot(k_ref[pl.ds(k0, FLASH_PIECE), :], qcat_ref[...],
                     preferred_element_type=F32)
        dst_ref[r * FLASH_PIECE:(r + 1) * FLASH_PIECE, :] = sc
        return jnp.max(sc, axis=0, keepdims=True)

    def probs(s_ref, r, mb, psum):
        for rr in range(FLASH_PIECE // BF16_ROWS):
            lo = r * FLASH_PIECE + rr * BF16_ROWS
            p0 = jnp.exp2(s_ref[lo:lo + SUBLANES, :] - mb)
            p1 = jnp.exp2(s_ref[lo + SUBLANES:lo + BF16_ROWS, :] - mb)
            psum = psum + p0 + p1
            p_ref[lo:lo + BF16_ROWS, :] = jnp.concatenate([p0, p1], axis=0).astype(BF16)
        return psum

    def pv_piece(ci, r):
        lo = r * FLASH_PIECE
        return jnp.dot(vt_ref.at[ci][:, lo:lo + FLASH_PIECE], p_ref[lo:lo + FLASH_PIECE, :],
                       preferred_element_type=F32)

    def body(j, carry):
        m, l, cmax = carry
        for b in range(unroll):
            ci = unroll * j + b
            nxt = jnp.minimum(ci + 1, nk - 1)
            m_new = jnp.maximum(m, cmax)
            alpha = jnp.exp2(m - m_new)
            mb = jnp.broadcast_to(m_new, (SUBLANES, ncols))
            psum = jnp.zeros((SUBLANES, ncols), F32)
            pv = None
            for r in range(n_pieces):
                cm = scores(nxt, r, s_bufs[(b + 1) % 2])
                cmax_next = cm if r == 0 else jnp.maximum(cmax_next, cm)
                psum = probs(s_bufs[b % 2], r, mb, psum)
                d = pv_piece(ci, r)
                pv = d if pv is None else pv + d
            acc_ref[...] = alpha * acc_ref[...] + pv
            l = alpha * l + jnp.sum(psum, axis=0, keepdims=True)
            m, cmax = m_new, cmax_next
        return m, l, cmax

    cmax0 = scores(0, 0, s_bufs[0])
    for r in range(1, n_pieces):
        cmax0 = jnp.maximum(cmax0, scores(0, r, s_bufs[0]))
    acc_ref[...] = jnp.zeros_like(acc_ref)
    init = (jnp.full((1, ncols), -jnp.inf, F32), jnp.zeros((1, ncols), F32), cmax0)
    _, l, _ = lax.fori_loop(0, nk // unroll, body, init)
    ot = acc_ref[...] * (1.0 / l)
    if mode == "single":
        o_ref[...] = ot.T.astype(BF16)
    elif mode == "pair":
        o_ref[:, :dv] = ot[:, :tq].T.astype(BF16)
        o_ref[:, dv:] = ot[:, tq:].T.astype(BF16)
    else:
        lp = lam_ref[...]
        lam = (jnp.exp(jnp.sum(lp[0:1] * lp[1:2], axis=1, keepdims=True))
               - jnp.exp(jnp.sum(lp[2:3] * lp[3:4], axis=1, keepdims=True)) + lam_init)
        d = (ot[:, :tq] - lam * ot[:, tq:]).T
        o_ref[...] = (_rms(d, subln_ref[...]) * (1.0 - lam_init)).astype(BF16)


def _flash(q_arr, k_arr, vt, *, mode, n_steps, q_blk, q_off, k_blk, k_off, out_w,
           tq=512, lam=None, subln=None, lam_init=0.0):
    s = q_arr.shape[0]
    tq = min(tq, s)
    _, nk, dv, tk = vt.shape
    unroll = math.gcd(nk, FLASH_UNROLL)
    assert unroll % 2 == 0
    ncols = tq if mode == "single" else 2 * tq
    in_specs = [pl.BlockSpec((tq, q_blk), lambda h, i: (i, h + q_off)),
                pl.BlockSpec((s, k_blk), lambda h, i: (0, h + k_off)),
                pl.BlockSpec((None, nk, dv, tk), lambda h, i: (h, 0, 0, 0))]
    args = [q_arr, k_arr, vt]
    if mode == "diff":
        in_specs += [pl.BlockSpec(lam.shape, lambda h, i: (0, 0)),
                     pl.BlockSpec((1, HEAD_DIM), lambda h, i: (0, 0))]
        args += [lam, subln.reshape(1, HEAD_DIM)]
    return pl.pallas_call(
        functools.partial(_flash_kernel, mode=mode, tq=tq, tk=tk, unroll=unroll,
                          lam_init=lam_init),
        grid=(n_steps, s // tq),
        in_specs=in_specs,
        out_specs=pl.BlockSpec((tq, out_w), lambda h, i: (i, h)),
        out_shape=jax.ShapeDtypeStruct((s, n_steps * out_w), BF16),
        scratch_shapes=[pltpu.VMEM((k_blk, ncols), BF16),
                        pltpu.VMEM((tk, ncols), F32), pltpu.VMEM((tk, ncols), F32),
                        pltpu.VMEM((tk, ncols), BF16),
                        pltpu.VMEM((dv, ncols), F32)],
        compiler_params=_cparams(2),
        name="flash_" + mode,
    )(*args)


def _swa_kernel(sink_ref, q_ref, k_ref, v_ref, o_ref, *, tq, sub, win, scale):
    g = pl.program_id(0)
    i = pl.program_id(1)
    s = k_ref.shape[0]
    group = SWA_HEADS // SWA_KV_HEADS
    for sb in range(tq // sub):
        rows = slice(sb * sub, (sb + 1) * sub)
        q0 = i * tq + sb * sub
        k0 = pl.multiple_of(jnp.clip(q0 - SWA_WINDOW, 0, s - win), LANES)
        kw = k_ref[pl.ds(k0, win), :]
        vw = v_ref[pl.ds(k0, win), :]
        qpos = q0 + lax.broadcasted_iota(jnp.int32, (sub, win), 0)
        kpos = k0 + lax.broadcasted_iota(jnp.int32, (sub, win), 1)
        valid = jnp.abs(kpos - qpos) <= SWA_WINDOW
        for j in range(group):
            cols = slice(j * HEAD_DIM, (j + 1) * HEAD_DIM)
            sc = lax.dot_general(q_ref[rows, cols], kw, (((1,), (1,)), ((), ())),
                                 preferred_element_type=F32) * scale
            sc = jnp.where(valid, sc, -jnp.inf)
            sink = sink_ref[g * group + j]
            m = jnp.maximum(jnp.max(sc, axis=1, keepdims=True), sink)
            p = jnp.exp(sc - m)
            denom = jnp.sum(p, axis=1, keepdims=True) + jnp.exp(sink - m)
            pr = (p * (1.0 / denom)).astype(BF16)
            o_ref[rows, cols] = jnp.dot(pr, vw, preferred_element_type=F32).astype(BF16)


def _swa(z, sink, tq=512, sub=256):
    s = z.shape[0]
    tq = min(tq, s)
    win = min(sub + 2 * SWA_WINDOW, s)
    group = SWA_HEADS // SWA_KV_HEADS
    qw = group * HEAD_DIM
    k_off = SWA_HEADS
    v_off = SWA_HEADS + SWA_KV_HEADS
    return pl.pallas_call(
        functools.partial(_swa_kernel, tq=tq, sub=sub, win=win, scale=HEAD_DIM ** -0.5),
        grid=(SWA_KV_HEADS, s // tq),
        in_specs=[pl.BlockSpec(memory_space=pltpu.SMEM),
                  pl.BlockSpec((tq, qw), lambda g, i: (i, g)),
                  pl.BlockSpec((s, HEAD_DIM), lambda g, i: (0, g + k_off)),
                  pl.BlockSpec((s, HEAD_DIM), lambda g, i: (0, g + v_off))],
        out_specs=pl.BlockSpec((tq, qw), lambda g, i: (i, g)),
        out_shape=jax.ShapeDtypeStruct((s, SWA_HEADS * HEAD_DIM), BF16),
        compiler_params=_cparams(2),
        name="swa",
    )(sink, z, z, z)


def _merge_kernel(h_ref, b0_ref, b1_ref, b2_ref, b3_ref, wg_ref, wb_ref, o_ref):
    h = h_ref[...]
    acc = None
    for j, b_ref in enumerate((b0_ref, b1_ref, b2_ref, b3_ref)):
        gate = _sigmoid(jnp.dot(h, wg_ref[j], preferred_element_type=F32))
        term = gate * jnp.dot(b_ref[...], wb_ref[j], preferred_element_type=F32)
        acc = term if acc is None else acc + term
    o_ref[...] = acc.astype(BF16)


def _merge(h, branches, wg, wb, layer, tm=512, tn=512):
    s, d = h.shape
    n = wg.shape[3]
    bw = wb.shape[2]
    tm = min(tm, s)
    b_spec = pl.BlockSpec((tm, bw), lambda j, i: (i, 0))
    return pl.pallas_call(
        _merge_kernel,
        grid=(n // tn, s // tm),
        in_specs=[pl.BlockSpec((tm, d), lambda j, i: (i, 0)),
                  b_spec, b_spec, b_spec, b_spec,
                  pl.BlockSpec((None, N_BRANCH, d, tn), lambda j, i: (layer, 0, 0, j)),
                  pl.BlockSpec((None, N_BRANCH, bw, tn), lambda j, i: (layer, 0, 0, j))],
        out_specs=pl.BlockSpec((tm, tn), lambda j, i: (i, j)),
        out_shape=jax.ShapeDtypeStruct((s, n), BF16),
        compiler_params=_cparams(2),
        name="merge",
    )(h, *branches, wg, wb)


def _attn_out_kernel(m_ref, w_ref, x_ref, gpost_ref, gnext_ref, xo_ref, ho_ref):
    y = jnp.dot(m_ref[...], w_ref[...], preferred_element_type=F32)
    xn = x_ref[...] + _rms(y, gpost_ref[...])
    xo_ref[...] = xn
    ho_ref[...] = _rms(xn, gnext_ref[...]).astype(BF16)


def _attn_out(merged, w, layer, x, gpost, gnext, tm=512):
    s, d = x.shape
    tm = min(tm, s)
    row = pl.BlockSpec((tm, d), lambda i: (i, 0))
    vec = pl.BlockSpec((1, d), lambda i: (0, 0))
    return pl.pallas_call(
        _attn_out_kernel,
        grid=(s // tm,),
        in_specs=[row, _resident((d, d), layer), row, vec, vec],
        out_specs=[row, row],
        out_shape=[jax.ShapeDtypeStruct((s, d), F32), jax.ShapeDtypeStruct((s, d), BF16)],
        compiler_params=_cparams(1),
        name="attn_out",
    )(merged, w, x, gpost.reshape(1, d), gnext.reshape(1, d))


def _ffn_in_kernel(h_ref, wg_ref, wu_ref, o_ref):
    h = h_ref[...]
    gate = jnp.dot(h, wg_ref[...], preferred_element_type=F32)
    up = jnp.dot(h, wu_ref[...], preferred_element_type=F32)
    o_ref[...] = (gate * _sigmoid(gate) * up).astype(BF16)


def _ffn_in(h, w, layer, tm=1024, tn=512):
    s, d = h.shape
    f = w.shape[2] // 2
    tm = min(tm, s)
    nt = f // tn
    return pl.pallas_call(
        _ffn_in_kernel,
        grid=(nt, s // tm),
        in_specs=[pl.BlockSpec((tm, d), lambda j, i: (i, 0)),
                  pl.BlockSpec((None, d, tn), lambda j, i: (layer, 0, j)),
                  pl.BlockSpec((None, d, tn), lambda j, i: (layer, 0, j + nt))],
        out_specs=pl.BlockSpec((tm, tn), lambda j, i: (i, j)),
        out_shape=jax.ShapeDtypeStruct((s, f), BF16),
        compiler_params=_cparams(2),
        name="ffn_in",
    )(h, w, w)


def _ffn_out_kernel(u_ref, w_ref, x_ref, g_ref, xo_ref):
    y = jnp.dot(u_ref[...], w_ref[...], preferred_element_type=F32)
    xo_ref[...] = x_ref[...] + _rms(y, g_ref[...])


def _ffn_out(u, w, layer, x, g, tm=256):
    s, d = x.shape
    f = u.shape[1]
    tm = min(tm, s)
    row = pl.BlockSpec((tm, d), lambda i: (i, 0))
    return pl.pallas_call(
        _ffn_out_kernel,
        grid=(s // tm,),
        in_specs=[pl.BlockSpec((tm, f), lambda i: (i, 0)),
                  _resident((f, d), layer),
                  row, pl.BlockSpec((1, d), lambda i: (0, 0))],
        out_specs=row,
        out_shape=jax.ShapeDtypeStruct((s, d), F32),
        compiler_params=_cparams(1),
        name="ffn_out",
    )(u, w, x, g.reshape(1, d))


def _ple_kernel(wg_ref, p_ref, wp_ref, x_ref, gpost_ref, *rest):
    x = x_ref[...]
    gate = _sigmoid(jnp.dot(x.astype(BF16), wg_ref[...], preferred_element_type=F32))
    emb = jnp.dot(p_ref[...].astype(BF16), wp_ref[...], preferred_element_type=F32)
    xn = x + _rms(gate * emb, gpost_ref[...])
    if len(rest) == 3:
        gnext_ref, xo_ref, ho_ref = rest
        ho_ref[...] = _rms(xn, gnext_ref[...]).astype(BF16)
    else:
        (xo_ref,) = rest
    xo_ref[...] = xn


def _ple(wg, p, wp, layer, x, gpost, gnext, tm=512):
    s, d = x.shape
    pd = p.shape[3]
    tm = min(tm, s)
    row = pl.BlockSpec((tm, d), lambda i: (i, 0))
    vec = pl.BlockSpec((1, d), lambda i: (0, 0))
    in_specs = [_resident((d, d), layer),
                pl.BlockSpec((None, None, tm, pd), lambda i: (layer, 0, i, 0)),
                _resident((pd, d), layer), row, vec]
    args = [wg, p, wp, x, gpost.reshape(1, d)]
    out_specs = [row]
    out_shape = [jax.ShapeDtypeStruct((s, d), F32)]
    if gnext is not None:
        in_specs.append(vec)
        args.append(gnext.reshape(1, d))
        out_specs.append(row)
        out_shape.append(jax.ShapeDtypeStruct((s, d), BF16))
    return pl.pallas_call(
        _ple_kernel,
        grid=(s // tm,),
        in_specs=in_specs,
        out_specs=out_specs,
        out_shape=out_shape,
        compiler_params=_cparams(1),
        name="ple",
    )(*args)


def _pad_cols(w, to):
    return jnp.pad(w, ((0, 0), (0, to - w.shape[1])))


def kernel(x, p, norm_mix_pre, norm_mix_post, norm_ffn_pre, norm_ffn_post, norm_ple_post, w_in, mla_qa_norm, mla_w_uq, mla_kva_norm, mla_w_ukv, diff_lambda, diff_subln, swa_sink, ax_q_norm, ax_k_norm, w_branch, w_branch_gate, w_o, w_ffn_in, w_ffn_out, w_ple, w_ple_gate):
    b, s, d = x.shape
    assert b == 1
    depth = w_in.shape[0]
    x2 = x.reshape(s, d)
    tab64, tab128, tab_ax = _position_tables(s)
    tk = min(FLASH_TK, s)

    c1 = MLA_Q_LORA + MLA_KV_LORA + MLA_ROPE

    t64, t128, tax = 0, 1, 2
    qs_b = DIFF_QK ** -0.5 * LOG2E
    qs_d = HEAD_DIM ** -0.5 * LOG2E
    plan = ([(t64, DIFF_QK // 2, None, qs_b, "zb", j) for j in range(4)]
            + [(t64, DIFF_QK // 2, None, 1.0, "zb", 4 + j) for j in range(4)]
            + [(None, 0, None, 1.0, "vtb", j) for j in range(4)]
            + [(t128, HEAD_DIM // 2, None, 1.0, "zc", j) for j in range(6)]
            + [(None, 0, None, 1.0, "zc", 6 + j) for j in range(2)]
            + [(tax, HEAD_DIM // 4, 0, qs_d, "zd", j) for j in range(4)]
            + [(tax, HEAD_DIM // 4, 1, 1.0, "zd", 4 + j) for j in range(2)]
            + [(None, 0, None, 1.0, "vtd", j) for j in range(2)])
    plan = tuple(plan)

    wg_all, wb_all = w_branch_gate.astype(BF16), w_branch.astype(BF16)
    wo_all = w_o.astype(BF16)
    wfi_all, wfo_all = w_ffn_in.astype(BF16), w_ffn_out.astype(BF16)
    wpg_all, wp_all = w_ple_gate.astype(BF16), w_ple.astype(BF16)

    h = _prenorm(x2, norm_mix_pre[0])
    for i in range(depth):
        lam_init = 0.8 - 0.6 * math.exp(-0.3 * i)
        wi = w_in[i].astype(BF16)
        w_mla = _pad_cols(wi[:, :c1], MLA_Q_LORA + MLA_KV_LORA + LANES)

        wq = jnp.pad(mla_w_uq[i].reshape(MLA_Q_LORA, MLA_HEADS, MLA_NOPE + MLA_ROPE),
                     ((0, 0), (0, 0), (0, MLA_QK_PAD - MLA_NOPE - MLA_ROPE))
                     ).reshape(MLA_Q_LORA, MLA_HEADS * MLA_QK_PAD).astype(BF16)
        wkv = mla_w_ukv[i].reshape(MLA_KV_LORA, MLA_HEADS, MLA_NOPE + MLA_V)
        wk = wkv[:, :, :MLA_NOPE].reshape(MLA_KV_LORA, MLA_HEADS * MLA_NOPE).astype(BF16)
        wv = wkv[:, :, MLA_NOPE:].reshape(MLA_KV_LORA, MLA_HEADS * MLA_V).astype(BF16)
        qa, ka, vta = _mla_proj(h, w_mla, wq, wk, wv, tab64, mla_qa_norm[i], mla_kva_norm[i], tk)
        br_a = _flash(qa, ka, vta, mode="single",
                      n_steps=MLA_HEADS, q_blk=MLA_QK_PAD, q_off=0, k_blk=MLA_QK_PAD, k_off=0,
                      out_w=MLA_V, tq=1024)

        gains_ax = jnp.stack([ax_q_norm[i], ax_k_norm[i]])
        zb, vtb, zc, zd, vtd = _inproj(h, wi[:, c1:], (tab64, tab128, tab_ax), gains_ax, plan, tk)

        br_b = _flash(zb, zb, vtb, mode="diff",
                      n_steps=DIFF_HEADS, q_blk=HEAD_DIM, q_off=0, k_blk=HEAD_DIM,
                      k_off=DIFF_HEADS, out_w=HEAD_DIM,
                      lam=diff_lambda[i], subln=diff_subln[i], lam_init=lam_init)

        br_c = _swa(zc, swa_sink[i])

        group = AX_HEADS // AX_KV_HEADS
        br_d = _flash(zd, zd, vtd, mode="pair",
                      n_steps=AX_KV_HEADS, q_blk=group * HEAD_DIM, q_off=0, k_blk=HEAD_DIM,
                      k_off=AX_HEADS, out_w=group * HEAD_DIM)

        merged = _merge(h, (br_a, br_b, br_c, br_d), wg_all, wb_all, i)
        x2, h2 = _attn_out(merged, wo_all, i, x2, norm_mix_post[i], norm_ffn_pre[i])

        u = _ffn_in(h2, wfi_all, i)
        x2 = _ffn_out(u, wfo_all, i, x2, norm_ffn_post[i])

        g_next = norm_mix_pre[i + 1] if i + 1 < depth else None
        outs = _ple(wpg_all, p, wp_all, i, x2, norm_ple_post[i], g_next)
        x2 = outs[0]
        h = outs[1] if g_next is not None else None
    return x2.reshape(b, s, d)
```
